```python
import math
import jax, jax.numpy as jnp
from jax import lax
import numpy as np

D_MODEL = 1024
BATCH = 2
SEQ = 8192
DEPTH = 4
DEC_BATCH = 128
DEC_SEQ = 8
PAST_LEN = 2048
PAGE_SIZE = 128

N_MEM = 256
Q_BLOCK = 128
H_FOX = 8
D_FOX = 64
W_FOX = H_FOX * D_FOX
H_DIFF = 4
D_DIFF = 64
W_DIFF = H_DIFF * 2 * D_DIFF
H_MEM = 4
D_MEM = 128
W_MEM = H_MEM * D_MEM
N_BRANCH = 3
IN_SPLIT = (W_FOX, W_FOX, W_FOX, W_FOX, H_FOX,
            W_DIFF, W_DIFF, W_DIFF, W_DIFF,
            W_MEM, W_MEM,
            N_BRANCH * D_MODEL)
VALUE_COLS = (2, 7)
ALPHA = (2 * DEPTH) ** 0.25
BETA = (8 * DEPTH) ** -0.25
FORGET_BIAS_INIT = 3.0
LN_EPS = 1e-5
RMS_EPS = 1e-5

kernel_name = 'fox_diff_mem_gated_hybrid_step'


def _heads(a, h):
    return a.reshape(*a.shape[:-1], h, a.shape[-1] // h)


def _layer_norm(x, g, b):
    xf = x.astype(jnp.float32)
    mu = jnp.mean(xf, axis=-1, keepdims=True)
    var = jnp.mean(jnp.square(xf - mu), axis=-1, keepdims=True)
    return ((xf - mu) * lax.rsqrt(var + LN_EPS) * g + b).astype(x.dtype)


def _rms_norm(x, g):
    xf = x.astype(jnp.float32)
    return (xf * lax.rsqrt(jnp.mean(xf * xf, axis=-1, keepdims=True) + RMS_EPS) * g).astype(x.dtype)


def _alibi_slopes():
    return 2.0 ** (-8.0 * jnp.arange(1, H_DIFF + 1, dtype=jnp.float32) / H_DIFF)


def _project(x, w_in, b_forget):
    parts = jnp.split(x @ w_in, np.cumsum(IN_SPLIT)[:-1].tolist(), axis=-1)
    fq, fk, fv, fg, ff, dq, dk, dv, dg, mq, mg, gates = parts
    logf = jax.nn.log_sigmoid((ff + b_forget).astype(jnp.float32))
    return (_heads(fq, H_FOX), _heads(fk, H_FOX), _heads(fv, H_FOX), fg, logf,
            _heads(dq, H_DIFF), _heads(dk, H_DIFF), _heads(dv, H_DIFF), dg,
            _heads(mq, H_MEM), mg, gates)


def _attn_probs(q, k, bias):
    s = jnp.einsum('bqhd,bkhd->bhqk', q, k).astype(jnp.float32) * (q.shape[-1] ** -0.5) + bias
    return jax.nn.softmax(s, axis=-1)


def _fox_attend(q, k, v, F, q_start):
    tq, L = q.shape[1], k.shape[1]
    pq = q_start + jnp.arange(tq)
    causal = jnp.arange(L)[None, :] <= pq[:, None]
    Fq = lax.dynamic_slice_in_dim(F, q_start, tq, axis=2)
    bias = jnp.where(causal, Fq[..., :, None] - F[..., None, :], -jnp.inf)
    p = _attn_probs(q, k, bias)
    return jnp.einsum('bhqk,bkhd->bqhd', p.astype(v.dtype), v)


def _diff_attend(q, k, v, q_start, lam, norm_g, lam_init):
    tq, L = q.shape[1], k.shape[1]
    dist = ((q_start + jnp.arange(tq))[:, None] - jnp.arange(L)[None, :]).astype(jnp.float32)
    bias = jnp.where(dist >= 0, -_alibi_slopes()[:, None, None] * dist, -jnp.inf)
    q1, q2 = jnp.split(q, 2, axis=-1)
    k1, k2 = jnp.split(k, 2, axis=-1)
    p = _attn_probs(q1, k1, bias) - lam * _attn_probs(q2, k2, bias)
    o = jnp.einsum('bhqk,bkhd->bqhd', p.astype(v.dtype), v)
    return _rms_norm(o, norm_g) * (1.0 - lam_init)


def _mem_attend(q, mk, mv):
    p = jax.nn.softmax(jnp.einsum('bqhd,bnhd->bhqn', q, mk).astype(jnp.float32) * (q.shape[-1] ** -0.5), axis=-1)
    o = jnp.einsum('bhqn,bnhd->bqhd', p.astype(mv.dtype), mv)
    return o.reshape(o.shape[0], o.shape[1], -1)


def _sweep_query_blocks(attend, q):
    b, s = q.shape[:2]

    def block(i):
        start = i * Q_BLOCK
        return attend(lax.dynamic_slice_in_dim(q, start, Q_BLOCK, axis=1), start)

    out = lax.map(block, jnp.arange(s // Q_BLOCK))
    return jnp.moveaxis(out, 0, 1).reshape(b, s, -1)


def _gather_pages(pool, l, page_table):
    g = pool[l, page_table]
    return g.reshape(page_table.shape[0], -1, *pool.shape[3:])


def _merge(x, o_fox, g_fox, o_diff, g_diff, o_mem, g_mem, gates, w_pf, w_pd, w_pm, w_o, ln_g, ln_b):
    s_fox, s_diff, s_mem = jnp.split(jax.nn.sigmoid(gates), N_BRANCH, axis=-1)
    merged = (s_fox * ((o_fox * jax.nn.silu(g_fox)) @ w_pf)
              + s_diff * ((o_diff * jax.nn.silu(g_diff)) @ w_pd)
              + s_mem * ((o_mem * jax.nn.silu(g_mem)) @ w_pm))
    return _layer_norm(ALPHA * x + merged @ w_o, ln_g, ln_b)


def setup_inputs(seed: int = 0) -> dict:
    key = jax.random.key(seed)
    ks = iter(jax.random.split(key, 48))
    n_pages = PAST_LEN // PAGE_SIZE
    n_used = DEC_BATCH * n_pages
    n_pool = n_used + n_used // 4

    def nrm(shape, scale=1.0):
        return jax.random.normal(next(ks), shape, jnp.float32) * scale

    x_prompt = nrm((BATCH, SEQ, D_MODEL))
    x_sample = nrm((DEC_BATCH, DEC_SEQ, D_MODEL))
    mem_prompt = nrm((BATCH, N_MEM, D_MODEL))
    cache_fox_k = nrm((DEPTH, n_pool, PAGE_SIZE, H_FOX, D_FOX))
    cache_fox_v = nrm((DEPTH, n_pool, PAGE_SIZE, H_FOX, D_FOX), BETA)
    cache_fox_logf = jax.nn.log_sigmoid(FORGET_BIAS_INIT + nrm((DEPTH, n_pool, PAGE_SIZE, H_FOX)))
    cache_diff_k = nrm((DEPTH, n_pool, PAGE_SIZE, H_DIFF, 2 * D_DIFF))
    cache_diff_v = nrm((DEPTH, n_pool, PAGE_SIZE, H_DIFF, 2 * D_DIFF), BETA)
    cache_mem_k = nrm((DEPTH, DEC_BATCH, N_MEM, H_MEM, D_MEM))
    cache_mem_v = nrm((DEPTH, DEC_BATCH, N_MEM, H_MEM, D_MEM), BETA)
    page_table = jax.random.permutation(next(ks), n_pool)[:n_used].reshape(DEC_BATCH, n_pages).astype(jnp.int32)
    w_in = jnp.concatenate(
        [nrm((DEPTH, D_MODEL, n), D_MODEL ** -0.5 * (BETA if i in VALUE_COLS else 1.0))
         for i, n in enumerate(IN_SPLIT)], axis=-1)
    b_forget = FORGET_BIAS_INIT + nrm((DEPTH, H_FOX), 0.1)
    w_mem_kv = jnp.concatenate([nrm((DEPTH, D_MODEL, W_MEM), D_MODEL ** -0.5),
                                nrm((DEPTH, D_MODEL, W_MEM), D_MODEL ** -0.5 * BETA)], axis=-1)
    lambda_q1 = nrm((DEPTH, D_DIFF), 0.1)
    lambda_k1 = nrm((DEPTH, D_DIFF), 0.1)
    lambda_q2 = nrm((DEPTH, D_DIFF), 0.1)
    lambda_k2 = nrm((DEPTH, D_DIFF), 0.1)
    diff_norm_g = 1.0 + nrm((DEPTH, 2 * D_DIFF), 0.02)
    w_proj_fox = nrm((DEPTH, W_FOX, D_MODEL), W_FOX ** -0.5 * BETA)
    w_proj_diff = nrm((DEPTH, W_DIFF, D_MODEL), W_DIFF ** -0.5 * BETA)
    w_proj_mem = nrm((DEPTH, W_MEM, D_MODEL), W_MEM ** -0.5 * BETA)
    w_out = nrm((DEPTH, D_MODEL, D_MODEL), D_MODEL ** -0.5 * BETA)
    ln_g = 1.0 + nrm((DEPTH, D_MODEL), 0.02)
    ln_b = nrm((DEPTH, D_MODEL), 0.02)
    return {'x_prompt': x_prompt, 'x_sample': x_sample, 'mem_prompt': mem_prompt,
            'cache_fox_k': cache_fox_k, 'cache_fox_v': cache_fox_v, 'cache_fox_logf': cache_fox_logf,
            'cache_diff_k': cache_diff_k, 'cache_diff_v': cache_diff_v,
            'cache_mem_k': cache_mem_k, 'cache_mem_v': cache_mem_v, 'page_table': page_table,
            'w_in': w_in, 'b_forget': b_forget, 'w_mem_kv': w_mem_kv,
            'lambda_q1': lambda_q1, 'lambda_k1': lambda_k1, 'lambda_q2': lambda_q2, 'lambda_k2': lambda_k2,
            'diff_norm_g': diff_norm_g, 'w_proj_fox': w_proj_fox, 'w_proj_diff': w_proj_diff,
            'w_proj_mem': w_proj_mem, 'w_out': w_out, 'ln_g': ln_g, 'ln_b': ln_b}


def reference(x_prompt, x_sample, mem_prompt, cache_fox_k, cache_fox_v, cache_fox_logf,
              cache_diff_k, cache_diff_v, cache_mem_k, cache_mem_v, page_table,
              w_in, b_forget, w_mem_kv, lambda_q1, lambda_k1, lambda_q2, lambda_k2,
              diff_norm_g, w_proj_fox, w_proj_diff, w_proj_mem, w_out, ln_g, ln_b):
    xp, xs = x_prompt, x_sample
    p_fk, p_fv, p_fl, p_dk, p_dv, p_mk, p_mv = [], [], [], [], [], [], []
    s_fk, s_fv, s_fl, s_dk, s_dv = [], [], [], [], []
    for l in range(DEPTH):
        lam_init = 0.8 - 0.6 * math.exp(-0.3 * l)
        lam = (jnp.exp(jnp.sum(lambda_q1[l] * lambda_k1[l]).astype(jnp.float32))
               - jnp.exp(jnp.sum(lambda_q2[l] * lambda_k2[l]).astype(jnp.float32)) + lam_init)
        out_w = (w_proj_fox[l], w_proj_diff[l], w_proj_mem[l], w_out[l], ln_g[l], ln_b[l])

        fq, fk, fv, fg, logf, dq, dk, dv, dg, mq, mg, gates = _project(xp, w_in[l], b_forget[l])
        mk, mv = jnp.split(mem_prompt @ w_mem_kv[l], 2, axis=-1)
        mk, mv = _heads(mk, H_MEM), _heads(mv, H_MEM)
        F = jnp.swapaxes(jnp.cumsum(logf, axis=1), 1, 2)
        o_fox = _sweep_query_blocks(lambda qb, st: _fox_attend(qb, fk, fv, F, st), fq)
        o_diff = _sweep_query_blocks(
            lambda qb, st: _diff_attend(qb, dk, dv, st, lam, diff_norm_g[l], lam_init), dq)
        o_mem = _mem_attend(mq, mk, mv)
        xp = _merge(xp, o_fox, fg, o_diff, dg, o_mem, mg, gates, *out_w)
        p_fk.append(fk); p_fv.append(fv); p_fl.append(logf)
        p_dk.append(dk); p_dv.append(dv); p_mk.append(mk); p_mv.append(mv)

        fq, fk, fv, fg, logf, dq, dk, dv, dg, mq, mg, gates = _project(xs, w_in[l], b_forget[l])
        b, t = xs.shape[0], xs.shape[1]
        fk_all = jnp.concatenate([_gather_pages(cache_fox_k, l, page_table), fk], axis=1)
        fv_all = jnp.concatenate([_gather_pages(cache_fox_v, l, page_table), fv], axis=1)
        logf_all = jnp.concatenate(
            [_gather_pages(cache_fox_logf, l, page_table).astype(jnp.float32), logf], axis=1)
        F = jnp.swapaxes(jnp.cumsum(logf_all, axis=1), 1, 2)
        o_fox = _fox_attend(fq, fk_all, fv_all, F, PAST_LEN).reshape(b, t, -1)
        dk_all = jnp.concatenate([_gather_pages(cache_diff_k, l, page_table), dk], axis=1)
        dv_all = jnp.concatenate([_gather_pages(cache_diff_v, l, page_table), dv], axis=1)
        o_diff = _diff_attend(dq, dk_all, dv_all, PAST_LEN, lam, diff_norm_g[l], lam_init).reshape(b, t, -1)
        o_mem = _mem_attend(mq, cache_mem_k[l], cache_mem_v[l])
        xs = _merge(xs, o_fox, fg, o_diff, dg, o_mem, mg, gates, *out_w)
        s_fk.append(fk); s_fv.append(fv); s_fl.append(logf); s_dk.append(dk); s_dv.append(dv)

    return (xp, xs,
            jnp.stack(p_fk), jnp.stack(p_fv), jnp.stack(p_fl), jnp.stack(p_dk), jnp.stack(p_dv),
            jnp.stack(p_mk), jnp.stack(p_mv),
            jnp.stack(s_fk), jnp.stack(s_fv), jnp.stack(s_fl), jnp.stack(s_dk), jnp.stack(s_dv))
```

```python
import functools
import math

import jax
import jax.numpy as jnp
from jax import lax
from jax.experimental import pallas as pl
from jax.experimental.pallas import tpu as pltpu

F32 = jnp.float32
BF16 = jnp.bfloat16

D_MODEL = 1024
H_FOX, D_FOX = 8, 64
H_DIFF, D_DIFF = 4, 64
H_MEM, D_MEM = 4, 128
W_BR = 512
N_BRANCH = 3
LN_EPS = 1e-5
RMS_EPS = 1e-5
LANES = 128
NEG = -1e30

VMEM_LIMIT = 48 * 1024 * 1024


def _log2(n):
    assert n & (n - 1) == 0, n
    return n.bit_length() - 1


def _cparams(sem):
    return pltpu.CompilerParams(dimension_semantics=sem, vmem_limit_bytes=VMEM_LIMIT)


def _mm_body(x_ref, w_ref, *o_refs):
    r = jnp.dot(x_ref[...].astype(BF16), w_ref[...], preferred_element_type=F32)
    for o in o_refs:
        o[...] = r.astype(o.dtype).reshape(o.shape)


def _matmul(x, w, tn, out_dtype, tm=512):
    m, k = x.shape
    n = w.shape[1]
    tm = min(tm, m)
    return pl.pallas_call(
        _mm_body,
        grid=(n // tn, m // tm),
        in_specs=[pl.BlockSpec((tm, k), lambda j, i: (i, 0)),
                  pl.BlockSpec((k, tn), lambda j, i: (0, j))],
        out_specs=pl.BlockSpec((tm, tn), lambda j, i: (i, j)),
        out_shape=jax.ShapeDtypeStruct((m, n), out_dtype),
        name="proj_act",
        compiler_params=_cparams(("parallel", "parallel")),
    )(x, w)


def _matmul_dual(x, w, tn, tm=512):
    m, k = x.shape
    n = w.shape[1]
    tm = min(tm, m)
    return pl.pallas_call(
        _mm_body,
        grid=(n // tn, m // tm),
        in_specs=[pl.BlockSpec((tm, k), lambda j, i: (i, 0)),
                  pl.BlockSpec((k, tn), lambda j, i: (0, j))],
        out_specs=[pl.BlockSpec((1, tm, tn), lambda j, i: (j, i, 0)),
                   pl.BlockSpec((tm, tn), lambda j, i: (i, j))],
        out_shape=[jax.ShapeDtypeStruct((n // tn, m, tn), F32),
                   jax.ShapeDtypeStruct((m, n), BF16)],
        name="proj_kv",
        compiler_params=_cparams(("parallel", "parallel")),
    )(x, w)


def _logf_body(x_ref, w_ref, b_ref, o_ref):
    z = jnp.dot(x_ref[...].astype(BF16), w_ref[...], preferred_element_type=F32) + b_ref[...]
    o_ref[...] = jnp.minimum(z, 0.0) - jnp.log1p(jnp.exp(-jnp.abs(z)))


def _logf_proj(x, w_pad, b_pad, tm=512):
    m, k = x.shape
    tm = min(tm, m)
    return pl.pallas_call(
        _logf_body,
        grid=(m // tm,),
        in_specs=[pl.BlockSpec((tm, k), lambda i: (i, 0)),
                  pl.BlockSpec((k, LANES), lambda i: (0, 0)),
                  pl.BlockSpec((1, LANES), lambda i: (0, 0))],
        out_specs=pl.BlockSpec((tm, LANES), lambda i: (i, 0)),
        out_shape=jax.ShapeDtypeStruct((m, LANES), F32),
        name="proj_logf",
        compiler_params=_cparams(("parallel",)),
    )(x, w_pad, b_pad)


def _dot01(a, b01):
    hi = a.astype(BF16)
    r1 = a - hi.astype(F32)
    mid = r1.astype(BF16)
    lo = (r1 - mid.astype(F32)).astype(BF16)
    d = functools.partial(jnp.dot, preferred_element_type=F32)
    return d(hi, b01) + d(mid, b01) + d(lo, b01)


def _upper_ones(n):
    r = lax.broadcasted_iota(jnp.int32, (n, n), 0)
    c = lax.broadcasted_iota(jnp.int32, (n, n), 1)
    return jnp.where(r <= c, 1.0, 0.0).astype(BF16)


def _cumsum_body(x_ref, o_ref):
    x = x_ref[0]
    rows = x.shape[0]
    within = _dot01(x, _upper_ones(LANES))
    r = lax.broadcasted_iota(jnp.int32, (rows, rows), 0)
    c = lax.broadcasted_iota(jnp.int32, (rows, rows), 1)
    strict_lower = jnp.where(c < r, 1.0, 0.0).astype(BF16)
    x_hi = x.astype(BF16)
    r1 = x - x_hi.astype(F32)
    x_mid = r1.astype(BF16)
    x_lo = (r1 - x_mid.astype(F32)).astype(BF16)
    d = functools.partial(jnp.dot, preferred_element_type=F32)
    prev = d(strict_lower, x_hi) + d(strict_lower, x_mid) + d(strict_lower, x_lo)
    all_ones = jnp.ones((LANES, LANES), BF16)
    o_ref[0] = within + _dot01(prev, all_ones)


def _cumsum_seq(x):
    n, t = x.shape
    rows = t // LANES
    x3 = x.reshape(n, rows, LANES)
    out = pl.pallas_call(
        _cumsum_body,
        grid=(n,),
        in_specs=[pl.BlockSpec((1, rows, LANES), lambda i: (i, 0, 0))],
        out_specs=pl.BlockSpec((1, rows, LANES), lambda i: (i, 0, 0)),
        out_shape=jax.ShapeDtypeStruct((n, rows, LANES), F32),
        name="logf_cumsum",
        compiler_params=_cparams(("parallel",)),
    )(x3)
    return out.reshape(n, t)


def _nt_dot(a, b):
    return lax.dot_general(a, b, (((1,), (1,)), ((), ())), preferred_element_type=F32)


def _softmax_step(s, m_ref, acc_ref, v_chunk):
    m_old = m_ref[...]
    m_new = jnp.maximum(m_old, jnp.max(s, axis=1, keepdims=True))
    alpha = jnp.exp(m_old - m_new)
    p = jnp.exp(s - m_new).astype(BF16)
    acc_ref[...] = alpha * acc_ref[...] + jnp.dot(p, v_chunk, preferred_element_type=F32)
    m_ref[...] = m_new


def _fox_body(q_ref, k_ref, v_ref, f_ref, o_ref, vaug_ref, acc_ref, m_ref, *, tq, tk):
    i = pl.program_id(2)
    t_all = k_ref.shape[1]
    half = D_FOX

    @pl.when(i == 0)
    def _():
        for c in range(t_all // tk):
            v = v_ref[0, c * tk:(c + 1) * tk, :].astype(F32)
            lane = lax.broadcasted_iota(jnp.int32, v.shape, 1)
            v0 = jnp.where(lane < half, v, jnp.where(lane == half, 1.0, 0.0))
            v1 = jnp.where(lane >= half, v, jnp.where(lane == 0, 1.0, 0.0))
            vaug_ref[0, c * tk:(c + 1) * tk, :] = v0.astype(BF16)
            vaug_ref[1, c * tk:(c + 1) * tk, :] = v1.astype(BF16)

    q = q_ref[0]
    qlane = lax.broadcasted_iota(jnp.int32, q.shape, 1)
    outs = []
    for hh in range(2):
        qh = jnp.where((qlane >= half) == (hh == 1), q, jnp.zeros_like(q))
        m_h = m_ref.at[hh]
        acc_h = acc_ref.at[hh]
        m_h[...] = jnp.full(m_h.shape, NEG, F32)
        acc_h[...] = jnp.zeros(acc_h.shape, F32)

        def chunk(j, masked, qh=qh, hh=hh, m_h=m_h, acc_h=acc_h):
            start = pl.multiple_of(j * tk, tk)
            s = _nt_dot(qh, k_ref[0, pl.ds(start, tk), :])
            s = s - f_ref[0, hh, pl.ds(j, 1), :]
            if masked:
                row = lax.broadcasted_iota(jnp.int32, s.shape, 0)
                col = lax.broadcasted_iota(jnp.int32, s.shape, 1)
                s = jnp.where(col <= row, s, NEG)
            _softmax_step(s, m_h, acc_h, vaug_ref[hh, pl.ds(start, tk), :])

        def full_chunk(j, carry, chunk=chunk):
            chunk(j, False)
            return carry

        lax.fori_loop(0, i, full_chunk, 0)
        chunk(i, True)
        acc = acc_h[...]
        l_col = acc[:, half:half + 1] if hh == 0 else acc[:, 0:1]
        outs.append(acc * (1.0 / l_col))
    lane = lax.broadcasted_iota(jnp.int32, outs[0].shape, 1)
    o_ref[0] = jnp.where(lane < half, outs[0], outs[1]).astype(o_ref.dtype)


def _fox_attention(qa, kva, fcum, b, t, q_col0, k_col0, v_col0, tq=512):
    tk = tq
    npair = H_FOX // 2
    return pl.pallas_call(
        functools.partial(_fox_body, tq=tq, tk=tk),
        grid=(b, npair, t // tq),
        in_specs=[pl.BlockSpec((1, tq, LANES), lambda bi, g, i: (bi, i, q_col0 + g)),
                  pl.BlockSpec((1, t, LANES), lambda bi, g, i: (bi, 0, k_col0 + g)),
                  pl.BlockSpec((1, t, LANES), lambda bi, g, i: (bi, 0, v_col0 + g)),
                  pl.BlockSpec((1, 2, t // tk, tk), lambda bi, g, i: (bi, g, 0, 0))],
        out_specs=pl.BlockSpec((1, tq, LANES), lambda bi, g, i: (bi, i, g)),
        out_shape=jax.ShapeDtypeStruct((b, t, W_BR), BF16),
        scratch_shapes=[pltpu.VMEM((2, t, LANES), BF16),
                        pltpu.VMEM((2, tq, LANES), F32),
                        pltpu.VMEM((2, tq, 1), F32)],
        name="fox_attn",
        compiler_params=_cparams(("parallel", "parallel", "arbitrary")),
    )(qa, kva, kva, fcum)


def _lambda_value(lam_ref, lam_init):
    lv = lam_ref[...]
    a = jnp.sum(lv[0:1] * lv[1:2], axis=1, keepdims=True)
    c = jnp.sum(lv[2:3] * lv[3:4], axis=1, keepdims=True)
    return jnp.exp(a) - jnp.exp(c) + lam_init


def _rms_scale(o, g, lam_init):
    ms = jnp.mean(o * o, axis=-1, keepdims=True)
    return o * lax.rsqrt(ms + RMS_EPS) * g * (1.0 - lam_init)


def _diff_body(slope_ref, q_ref, k_ref, v_ref, lam_ref, g_ref, o_ref,
               vaug_ref, acc_ref, m_ref, *, tq, tk, lam_init):
    h = pl.program_id(1)
    i = pl.program_id(2)
    t_all = k_ref.shape[1]
    half = D_DIFF
    dv = 2 * D_DIFF

    @pl.when(i == 0)
    def _():
        for c in range(t_all // tk):
            sl = slice(c * tk, (c + 1) * tk)
            vaug_ref[sl, :dv] = v_ref[0, sl, :]
            lane = lax.broadcasted_iota(jnp.int32, (tk, LANES), 1)
            vaug_ref[sl, dv:] = jnp.where(lane == 0, 1.0, 0.0).astype(BF16)

    slope = slope_ref[h]
    q = q_ref[0]
    qlane = lax.broadcasted_iota(jnp.int32, q.shape, 1)
    outs = []
    for mm in range(2):
        qm = jnp.where((qlane >= half) == (mm == 1), q, jnp.zeros_like(q))
        m_m = m_ref.at[mm]
        acc_m = acc_ref.at[mm]
        m_m[...] = jnp.full(m_m.shape, NEG, F32)
        acc_m[...] = jnp.zeros(acc_m.shape, F32)

        def chunk(j, masked, qm=qm, m_m=m_m, acc_m=acc_m):
            start = pl.multiple_of(j * tk, tk)
            s = _nt_dot(qm, k_ref[0, pl.ds(start, tk), :])
            rel = lax.broadcasted_iota(jnp.int32, (1, tk), 1) + (j * tk - i * tq)
            s = s + slope * rel.astype(F32)
            if masked:
                row = lax.broadcasted_iota(jnp.int32, s.shape, 0)
                col = lax.broadcasted_iota(jnp.int32, s.shape, 1)
                s = jnp.where(col <= row, s, NEG)
            _softmax_step(s, m_m, acc_m, vaug_ref[pl.ds(start, tk), :])

        def full_chunk(j, carry, chunk=chunk):
            chunk(j, False)
            return carry

        lax.fori_loop(0, i, full_chunk, 0)
        chunk(i, True)
        acc = acc_m[...]
        outs.append(acc[:, :dv] * (1.0 / acc[:, dv:dv + 1]))
    lam = _lambda_value(lam_ref, lam_init)
    o = outs[0] - lam * outs[1]
    o_ref[0] = _rms_scale(o, g_ref[...], lam_init).astype(o_ref.dtype)


def _diff_attention(qa, kva, slopes, lamv, g, b, t, q_col0, k_col0, v_col0, lam_init, tq=512):
    tk = tq
    dv = 2 * D_DIFF
    return pl.pallas_call(
        functools.partial(_diff_body, tq=tq, tk=tk, lam_init=lam_init),
        grid=(b, H_DIFF, t // tq),
        in_specs=[pl.BlockSpec(memory_space=pltpu.SMEM),
                  pl.BlockSpec((1, tq, LANES), lambda bi, h, i: (bi, i, q_col0 + h)),
                  pl.BlockSpec((1, t, LANES), lambda bi, h, i: (bi, 0, k_col0 + h)),
                  pl.BlockSpec((1, t, LANES), lambda bi, h, i: (bi, 0, v_col0 + h)),
                  pl.BlockSpec((4, D_DIFF), lambda bi, h, i: (0, 0)),
                  pl.BlockSpec((1, dv), lambda bi, h, i: (0, 0))],
        out_specs=pl.BlockSpec((1, tq, LANES), lambda bi, h, i: (bi, i, h)),
        out_shape=jax.ShapeDtypeStruct((b, t, W_BR), BF16),
        scratch_shapes=[pltpu.VMEM((t, dv + LANES), BF16),
                        pltpu.VMEM((2, tq, dv + LANES), F32),
                        pltpu.VMEM((2, tq, 1), F32)],
        name="diff_attn",
        compiler_params=_cparams(("parallel", "parallel", "arbitrary")),
    )(slopes, qa, kva, kva, lamv, g)


def _mem_body(q_ref, k_ref, v_ref, o_ref):
    scale = D_MEM ** -0.5
    for h in range(H_MEM):
        sl = slice(h * D_MEM, (h + 1) * D_MEM)
        s = _nt_dot(q_ref[0, :, sl], k_ref[0, :, sl]) * scale
        m = jnp.max(s, axis=1, keepdims=True)
        p = jnp.exp(s - m)
        l = jnp.sum(p, axis=1, keepdims=True)
        o = jnp.dot(p.astype(BF16), v_ref[0, :, sl], preferred_element_type=F32)
        o_ref[0, :, sl] = (o * (1.0 / l)).astype(o_ref.dtype)


def _mem_attention(qa, mkv, b, t, q_col0, tq=512):
    n_mem = mkv.shape[1]
    return pl.pallas_call(
        _mem_body,
        grid=(b, t // tq),
        in_specs=[pl.BlockSpec((1, tq, W_BR), lambda bi, i: (bi, i, q_col0)),
                  pl.BlockSpec((1, n_mem, W_BR), lambda bi, i: (bi, 0, 0)),
                  pl.BlockSpec((1, n_mem, W_BR), lambda bi, i: (bi, 0, 1))],
        out_specs=pl.BlockSpec((1, tq, W_BR), lambda bi, i: (bi, i, 0)),
        out_shape=jax.ShapeDtypeStruct((b, t, W_BR), BF16),
        name="mem_attn",
        compiler_params=_cparams(("parallel", "parallel")),
    )(qa, mkv, mkv)


def _silu_gate(o_ref, g_ref):
    g = g_ref[...].astype(F32)
    return (o_ref[...].astype(F32) * (g * jax.nn.sigmoid(g))).astype(BF16)


def _merge_body(x_ref, of_ref, od_ref, om_ref, fg_ref, dg_ref, mg_ref, gates_ref,
                wpf_ref, wpd_ref, wpm_ref, wo_ref, lng_ref, lnb_ref, y_ref, *, alpha):
    d = functools.partial(jnp.dot, preferred_element_type=F32)
    merged = None
    for n, (o_ref, g_ref, w_ref) in enumerate(((of_ref, fg_ref, wpf_ref),
                                               (od_ref, dg_ref, wpd_ref),
                                               (om_ref, mg_ref, wpm_ref))):
        gate = jax.nn.sigmoid(gates_ref[:, n * D_MODEL:(n + 1) * D_MODEL].astype(F32))
        term = gate * d(_silu_gate(o_ref, g_ref), w_ref[...])
        merged = term if merged is None else merged + term
    z = alpha * x_ref[...] + d(merged.astype(BF16), wo_ref[...])
    mu = jnp.mean(z, axis=-1, keepdims=True)
    zc = z - mu
    var = jnp.mean(zc * zc, axis=-1, keepdims=True)
    y_ref[...] = zc * lax.rsqrt(var + LN_EPS) * lng_ref[...] + lnb_ref[...]


def _merge(x, o_fox, o_diff, o_mem, proj, cols, wpf, wpd, wpm, wo, ln_g, ln_b, alpha, tm=512):
    m = x.shape[0]
    tm = min(tm, m)
    fg_c, dg_c, mg_c, gates_c = cols
    row = lambda i: (i, 0)
    const = lambda i: (0, 0)
    return pl.pallas_call(
        functools.partial(_merge_body, alpha=alpha),
        grid=(m // tm,),
        in_specs=[pl.BlockSpec((tm, D_MODEL), row),
                  pl.BlockSpec((tm, W_BR), row),
                  pl.BlockSpec((tm, W_BR), row),
                  pl.BlockSpec((tm, W_BR), row),
                  pl.BlockSpec((tm, W_BR), lambda i: (i, fg_c)),
                  pl.BlockSpec((tm, W_BR), lambda i: (i, dg_c)),
                  pl.BlockSpec((tm, W_BR), lambda i: (i, mg_c)),
                  pl.BlockSpec((tm, N_BRANCH * D_MODEL), lambda i: (i, gates_c)),
                  pl.BlockSpec((W_BR, D_MODEL), const),
                  pl.BlockSpec((W_BR, D_MODEL), const),
                  pl.BlockSpec((W_BR, D_MODEL), const),
                  pl.BlockSpec((D_MODEL, D_MODEL), const),
                  pl.BlockSpec((1, D_MODEL), const),
                  pl.BlockSpec((1, D_MODEL), const)],
        out_specs=pl.BlockSpec((tm, D_MODEL), row),
        out_shape=jax.ShapeDtypeStruct((m, D_MODEL), F32),
        name="merge",
        compiler_params=_cparams(("parallel",)),
    )(x, o_fox, o_diff, o_mem, proj, proj, proj, proj, wpf, wpd, wpm, wo, ln_g, ln_b)


def _sample_softmax_step(s, m_ref, l_ref, acc_ref, v):
    m_old = m_ref[...]
    m_new = jnp.maximum(m_old, jnp.max(s, axis=1, keepdims=True))
    alpha = jnp.exp(m_old - m_new)
    p = jnp.exp(s - m_new)
    l_ref[...] = alpha * l_ref[...] + jnp.sum(p, axis=1, keepdims=True)
    acc_ref[...] = alpha * acc_ref[...] + jnp.dot(p.astype(BF16), v, preferred_element_type=F32)
    m_ref[...] = m_new


def _sample_body(pt_ref, qf_ref, qd_ref, qm_ref,
                 cfk_ref, cfv_ref, cfl_ref, cdk_ref, cdv_ref,
                 nfk_ref, nfv_ref, nfl_ref, ndk_ref, ndv_ref,
                 mk_ref, mv_ref, slope_ref, lam_ref, g_ref,
                 of_ref, od_ref, om_ref,
                 mf_ref, lf_ref, af_ref, md_ref, ld_ref, ad_ref, fc_ref,
                 *, n_pages, page, t_new, lam_init):
    del pt_ref
    p_id = pl.program_id(1)
    rows = qf_ref.shape[1]
    nsub = rows // t_new

    @pl.when(p_id == 0)
    def _():
        mf_ref[...] = jnp.full(mf_ref.shape, NEG, F32)
        md_ref[...] = jnp.full(md_ref.shape, NEG, F32)
        for r in (lf_ref, af_ref, ld_ref, ad_ref, fc_ref):
            r[...] = jnp.zeros(r.shape, F32)
        qm = qm_ref[0]
        s = _nt_dot(qm, mk_ref[0, 0].astype(BF16)) * (D_MEM ** -0.5)
        m = jnp.max(s, axis=1, keepdims=True)
        p = jnp.exp(s - m)
        l = jnp.sum(p, axis=1, keepdims=True)
        o = jnp.dot(p.astype(BF16), mv_ref[0, 0].astype(BF16), preferred_element_type=F32) * (1.0 / l)
        o3 = o.reshape(H_MEM, t_new, W_BR)
        hidx = lax.broadcasted_iota(jnp.int32, o3.shape, 0)
        cidx = lax.broadcasted_iota(jnp.int32, o3.shape, 2) >> _log2(D_MEM)
        om_ref[0] = jnp.sum(jnp.where(hidx == cidx, o3, 0.0), axis=0).astype(om_ref.dtype)

    def step(fk, fv, fl, dk, dv, pos0, new):
        row_t = lax.broadcasted_iota(jnp.int32, (rows, page), 0) >> _log2(nsub)
        col = lax.broadcasted_iota(jnp.int32, (rows, page), 1)
        valid = col <= row_t
        fcum = fc_ref[...] + _dot01(fl, _upper_ones(page))
        fc_ref[...] = jnp.broadcast_to(fcum[:, page - 1:page], fc_ref.shape)
        s = _nt_dot(qf_ref[0], fk)
        s = (s.reshape(t_new, nsub, page) - fcum[None]).reshape(rows, page)
        if new:
            s = jnp.where(valid, s, NEG)
        _sample_softmax_step(s, mf_ref, lf_ref, af_ref, fv)
        s = _nt_dot(qd_ref[0], dk)
        rel = (lax.broadcasted_iota(jnp.int32, (nsub, page), 1) + pos0).astype(F32)
        s = (s.reshape(t_new, nsub, page) + (slope_ref[...] * rel)[None]).reshape(rows, page)
        if new:
            s = jnp.where(valid, s, NEG)
        _sample_softmax_step(s, md_ref, ld_ref, ad_ref, dv)

    @pl.when(p_id < n_pages)
    def _():
        step(cfk_ref[0, 0].astype(BF16), cfv_ref[0, 0].astype(BF16), cfl_ref[0, 0],
             cdk_ref[0, 0].astype(BF16), cdv_ref[0, 0].astype(BF16),
             p_id * page - n_pages * page, False)

    @pl.when(p_id == n_pages)
    def _():
        step(nfk_ref[0], nfv_ref[0], nfl_ref[0], ndk_ref[0], ndv_ref[0], 0, True)
        o3 = (af_ref[...] * (1.0 / lf_ref[...])).reshape(t_new, nsub, W_BR)
        hidx = lax.broadcasted_iota(jnp.int32, o3.shape, 1)
        cidx = lax.broadcasted_iota(jnp.int32, o3.shape, 2)
        of_ref[0] = jnp.sum(jnp.where(hidx == cidx >> _log2(D_FOX), o3, 0.0), axis=1).astype(of_ref.dtype)
        lam = _lambda_value(lam_ref, lam_init)
        o3 = (ad_ref[...] * (1.0 / ld_ref[...])).reshape(t_new, nsub, W_BR)
        own = (hidx & (H_DIFF - 1)) == cidx >> _log2(2 * D_DIFF)
        wgt = jnp.where(hidx < H_DIFF, 1.0, -lam)
        od = jnp.sum(jnp.where(own, o3 * wgt, 0.0), axis=1)
        for h in range(H_DIFF):
            sl = slice(h * 2 * D_DIFF, (h + 1) * 2 * D_DIFF)
            od_ref[0, :, sl] = _rms_scale(od[:, sl], g_ref[...], lam_init).astype(od_ref.dtype)


def _sample_attention(l, page_table, qf, qd, qm, caches, news, mem_k, mem_v, slope8, lamv, g, lam_init):
    cfk, cfv, cfl, cdk, cdv = caches
    nfk, nfv, nfl, ndk, ndv = news
    n_seq, n_pages = page_table.shape
    page = cfk.shape[2]
    rows = qf.shape[1]
    t_new = rows // 8
    n_mem = mem_k.shape[2]

    def cache_map(s, p, pt):
        return (l, pt[s * n_pages + jnp.minimum(p, n_pages - 1)], 0, 0)

    seq3 = lambda s, p, pt: (s, 0, 0)
    const2 = lambda s, p, pt: (0, 0)
    cache_spec = pl.BlockSpec((1, 1, page, W_BR), cache_map)
    new_spec = pl.BlockSpec((1, page, W_BR), seq3)
    out_spec = pl.BlockSpec((1, t_new, W_BR), seq3)
    grid_spec = pltpu.PrefetchScalarGridSpec(
        num_scalar_prefetch=1,
        grid=(n_seq, n_pages + 1),
        in_specs=[pl.BlockSpec((1, rows, W_BR), seq3),
                  pl.BlockSpec((1, rows, W_BR), seq3),
                  pl.BlockSpec((1, H_MEM * t_new, W_BR), seq3),
                  cache_spec, cache_spec,
                  pl.BlockSpec((1, 1, 8, page), cache_map),
                  cache_spec, cache_spec,
                  new_spec, new_spec,
                  pl.BlockSpec((1, 8, page), seq3),
                  new_spec, new_spec,
                  pl.BlockSpec((1, 1, n_mem, W_BR), lambda s, p, pt: (l, s, 0, 0)),
                  pl.BlockSpec((1, 1, n_mem, W_BR), lambda s, p, pt: (l, s, 0, 0)),
                  pl.BlockSpec((8, page), const2),
                  pl.BlockSpec((4, D_DIFF), const2),
                  pl.BlockSpec((1, 2 * D_DIFF), const2)],
        out_specs=[out_spec, out_spec, out_spec],
        scratch_shapes=[pltpu.VMEM((rows, 1), F32), pltpu.VMEM((rows, 1), F32),
                        pltpu.VMEM((rows, W_BR), F32),
                        pltpu.VMEM((rows, 1), F32), pltpu.VMEM((rows, 1), F32),
                        pltpu.VMEM((rows, W_BR), F32),
                        pltpu.VMEM((8, page), F32)])
    shp = jax.ShapeDtypeStruct((n_seq, t_new, W_BR), BF16)
    return pl.pallas_call(
        functools.partial(_sample_body, n_pages=n_pages, page=page, t_new=t_new, lam_init=lam_init),
        grid_spec=grid_spec,
        out_shape=[shp, shp, shp],
        name="sample_attn",
        compiler_params=_cparams(("parallel", "arbitrary")),
    )(page_table.reshape(-1), qf, qd, qm, cfk, cfv, cfl, cdk, cdv,
      nfk, nfv, nfl, ndk, ndv, mem_k, mem_v, slope8, lamv, g)


_A_FQ, _A_FG, _A_DQ, _A_DG, _A_MQ, _A_MG = range(6)
_A_GATES = 6 * W_BR
_KV_FK, _KV_FV, _KV_DK, _KV_DV = range(4)


def _split_w_in(w):
    sizes = (W_BR, W_BR, W_BR, W_BR, H_FOX, W_BR, W_BR, W_BR, W_BR, W_BR, W_BR, N_BRANCH * D_MODEL)
    offs = [0]
    for s in sizes:
        offs.append(offs[-1] + s)
    fq, fk, fv, fg, ff, dq, dk, dv, dg, mq, mg, gates = [w[:, offs[n]:offs[n + 1]] for n in range(12)]
    w_act = jnp.concatenate([fq * (D_FOX ** -0.5), fg, dq * (D_DIFF ** -0.5), dg, mq, mg, gates],
                            axis=1).astype(BF16)
    w_kv = jnp.concatenate([fk, fv, dk, dv], axis=1).astype(BF16)
    w_ff = jnp.pad(ff, ((0, 0), (0, LANES - H_FOX))).astype(BF16)
    return w_act, w_kv, w_ff


def _block_diag_q(q, n_head, order_head_major):
    s, t, w = q.shape
    d = w // n_head
    q4 = q.reshape(s, t, 1, n_head, d)
    eye = jnp.eye(n_head, dtype=q.dtype).reshape(1, 1, n_head, n_head, 1)
    bd = q4 * eye
    if order_head_major:
        bd = jnp.swapaxes(bd, 1, 2)
    return bd.reshape(s, t * n_head, w)


def kernel(x_prompt, x_sample, mem_prompt, cache_fox_k, cache_fox_v, cache_fox_logf, cache_diff_k,
           cache_diff_v, cache_mem_k, cache_mem_v, page_table, w_in, b_forget, w_mem_kv, lambda_q1,
           lambda_k1, lambda_q2, lambda_k2, diff_norm_g, w_proj_fox, w_proj_diff, w_proj_mem, w_out,
           ln_g, ln_b):
    depth = w_in.shape[0]
    b, t, _ = x_prompt.shape
    n_seq, t_new, _ = x_sample.shape
    n_pool, page = cache_fox_k.shape[1], cache_fox_k.shape[2]
    n_pages = page_table.shape[1]
    n_mem = mem_prompt.shape[1]
    alpha = (2 * depth) ** 0.25
    tq = min(512, t)

    slopes = 2.0 ** (-8.0 * jnp.arange(1, H_DIFF + 1, dtype=F32) / H_DIFF)
    slope8 = jnp.broadcast_to(jnp.tile(slopes, 2)[:, None], (8, page))

    cfk = cache_fox_k.reshape(depth, n_pool, page, W_BR)
    cfv = cache_fox_v.reshape(depth, n_pool, page, W_BR)
    cdk = cache_diff_k.reshape(depth, n_pool, page, W_BR)
    cdv = cache_diff_v.reshape(depth, n_pool, page, W_BR)
    cfl = jnp.swapaxes(cache_fox_logf, 2, 3)
    cmk = cache_mem_k.reshape(depth, n_seq, n_mem, W_BR)
    cmv = cache_mem_v.reshape(depth, n_seq, n_mem, W_BR)

    xp = x_prompt.reshape(b * t, D_MODEL)
    xs = x_sample.reshape(n_seq * t_new, D_MODEL)
    mem2 = mem_prompt.reshape(b * n_mem, D_MODEL)
    outs = [[] for _ in range(12)]

    for l in range(depth):
        lam_init = 0.8 - 0.6 * math.exp(-0.3 * l)
        w_act, w_kv, w_ff = _split_w_in(w_in[l])
        b_ff = jnp.pad(b_forget[l], (0, LANES - H_FOX)).reshape(1, LANES)
        lamv = jnp.stack([lambda_q1[l], lambda_k1[l], lambda_q2[l], lambda_k2[l]])
        g = diff_norm_g[l].reshape(1, 2 * D_DIFF)
        merge_w = (w_proj_fox[l].astype(BF16), w_proj_diff[l].astype(BF16), w_proj_mem[l].astype(BF16),
                   w_out[l].astype(BF16), ln_g[l].reshape(1, D_MODEL), ln_b[l].reshape(1, D_MODEL))
        merge_cols = (_A_FG, _A_DG, _A_MG, _A_GATES // (N_BRANCH * D_MODEL))

        act = _matmul(xp, w_act, 1536, BF16)
        kv32, kv16 = _matmul_dual(xp, w_kv, W_BR)
        logf = _logf_proj(xp, w_ff, b_ff)[:, :H_FOX]
        mkv32, mkv16 = _matmul_dual(mem2, w_mem_kv[l].astype(BF16), W_BR)
        fcum = _cumsum_seq(jnp.swapaxes(logf.reshape(b, t, H_FOX), 1, 2).reshape(b * H_FOX, t))
        act3 = act.reshape(b, t, -1)
        kv3 = kv16.reshape(b, t, -1)
        nblk = W_BR // LANES
        o_fox = _fox_attention(act3, kv3, fcum.reshape(b, H_FOX, t // tq, tq), b, t,
                               _A_FQ * nblk, _KV_FK * nblk, _KV_FV * nblk, tq=tq)
        o_diff = _diff_attention(act3, kv3, slopes, lamv, g, b, t,
                                 _A_DQ * nblk, _KV_DK * nblk, _KV_DV * nblk, lam_init, tq=tq)
        o_mem = _mem_attention(act3, mkv16.reshape(b, n_mem, 2 * W_BR), b, t, _A_MQ, tq=tq)
        xp = _merge(xp, o_fox.reshape(b * t, W_BR), o_diff.reshape(b * t, W_BR),
                    o_mem.reshape(b * t, W_BR), act, merge_cols, *merge_w, alpha)
        outs[0].append(kv32[_KV_FK].reshape(b, t, H_FOX, D_FOX))
        outs[1].append(kv32[_KV_FV].reshape(b, t, H_FOX, D_FOX))
        outs[2].append(logf.reshape(b, t, H_FOX))
        outs[3].append(kv32[_KV_DK].reshape(b, t, H_DIFF, 2 * D_DIFF))
        outs[4].append(kv32[_KV_DV].reshape(b, t, H_DIFF, 2 * D_DIFF))
        outs[5].append(mkv32[0].reshape(b, n_mem, H_MEM, D_MEM))
        outs[6].append(mkv32[1].reshape(b, n_mem, H_MEM, D_MEM))

        act = _matmul(xs, w_act, 1536, BF16)
        kv32, kv16 = _matmul_dual(xs, w_kv, W_BR)
        logf = _logf_proj(xs, w_ff, b_ff)[:, :H_FOX]
        act3 = act.reshape(n_seq, t_new, -1)
        qf = _block_diag_q(act3[..., _A_FQ * W_BR:(_A_FQ + 1) * W_BR], H_FOX, False)
        dq = act3[..., _A_DQ * W_BR:(_A_DQ + 1) * W_BR].reshape(n_seq, t_new, H_DIFF, 2, D_DIFF)
        col_map = jnp.arange(2).reshape(1, 1, 1, 1, 2, 1)
        col_head = jnp.arange(H_DIFF).reshape(1, 1, 1, H_DIFF, 1, 1)
        row_map = jnp.arange(2).reshape(1, 2, 1, 1, 1, 1)
        row_head = jnp.arange(H_DIFF).reshape(1, 1, H_DIFF, 1, 1, 1)
        sel = ((col_map == row_map) & (col_head == row_head)).astype(BF16)
        qd = (dq[:, :, None, None] * sel[None]).reshape(n_seq, t_new * 8, W_BR)
        qm = _block_diag_q(act3[..., _A_MQ * W_BR:(_A_MQ + 1) * W_BR], H_MEM, True)
        pad_rows = lambda a: jnp.pad(a.reshape(n_seq, t_new, W_BR), ((0, 0), (0, page - t_new), (0, 0)))
        kv4 = kv16.reshape(n_seq * t_new, 4, W_BR)
        nfl = jnp.pad(jnp.swapaxes(logf.reshape(n_seq, t_new, H_FOX), 1, 2),
                      ((0, 0), (0, 0), (0, page - t_new)))
        news = (pad_rows(kv4[:, _KV_FK]), pad_rows(kv4[:, _KV_FV]), nfl,
                pad_rows(kv4[:, _KV_DK]), pad_rows(kv4[:, _KV_DV]))
        o_fox, o_diff, o_mem = _sample_attention(
            l, page_table, qf, qd, qm, (cfk, cfv, cfl, cdk, cdv), news, cmk, cmv,
            slope8, lamv, g, lam_init)
        m_s = n_seq * t_new
        xs = _merge(xs, o_fox.reshape(m_s, W_BR), o_diff.reshape(m_s, W_BR), o_mem.reshape(m_s, W_BR),
                    act, merge_cols, *merge_w, alpha)
        outs[7].append(kv32[_KV_FK].reshape(n_seq, t_new, H_FOX, D_FOX))
        outs[8].append(kv32[_KV_FV].reshape(n_seq, t_new, H_FOX, D_FOX))
        outs[9].append(logf.reshape(n_seq, t_new, H_FOX))
        outs[10].append(kv32[_KV_DK].reshape(n_seq, t_new, H_DIFF, 2 * D_DIFF))
        outs[11].append(kv32[_KV_DV].reshape(n_seq, t_new, H_DIFF, 2 * D_DIFF))

    return (xp.reshape(b, t, D_MODEL), xs.reshape(n_seq, t_new, D_MODEL)) + tuple(
        jnp.stack(o) for o in outs)
```

```python
import functools
import math

import jax
import jax.numpy as jnp
from jax import lax
from jax.experimental import pallas as pl
from jax.experimental.pallas import tpu as pltpu

F32 = jnp.float32
BF16 = jnp.bfloat16

D_MODEL = 1024
H_FOX, D_FOX = 8, 64
H_DIFF, D_DIFF = 4, 64
H_MEM, D_MEM = 4, 128
W_BR = 512
N_BRANCH = 3
LN_EPS = 1e-5
RMS_EPS = 1e-5
LANES = 128
NEG = -1e30

VMEM_LIMIT = 48 * 1024 * 1024


def _log2(n):
    assert n & (n - 1) == 0, n
    return n.bit_length() - 1


def _cparams(sem):
    return pltpu.CompilerParams(dimension_semantics=sem, vmem_limit_bytes=VMEM_LIMIT)


def _mm_body(x_ref, w_ref, *o_refs):
    r = jnp.dot(x_ref[...].astype(BF16), w_ref[...], preferred_element_type=F32)
    for o in o_refs:
        o[...] = r.astype(o.dtype).reshape(o.shape)


def _matmul(x, w, tn, out_dtype, tm=512):
    m, k = x.shape
    n = w.shape[1]
    tm = min(tm, m)
    return pl.pallas_call(
        _mm_body,
        grid=(n // tn, m // tm),
        in_specs=[pl.BlockSpec((tm, k), lambda j, i: (i, 0)),
                  pl.BlockSpec((k, tn), lambda j, i: (0, j))],
        out_specs=pl.BlockSpec((tm, tn), lambda j, i: (i, j)),
        out_shape=jax.ShapeDtypeStruct((m, n), out_dtype),
        name="proj",
        compiler_params=_cparams(("parallel", "parallel")),
    )(x, w)


def _matmul_dual(x, w, tn, tm=512):
    m, k = x.shape
    n = w.shape[1]
    tm = min(tm, m)
    return pl.pallas_call(
        _mm_body,
        grid=(n // tn, m // tm),
        in_specs=[pl.BlockSpec((tm, k), lambda j, i: (i, 0)),
                  pl.BlockSpec((k, tn), lambda j, i: (0, j))],
        out_specs=[pl.BlockSpec((1, tm, tn), lambda j, i: (j, i, 0)),
                   pl.BlockSpec((tm, tn), lambda j, i: (i, j))],
        out_shape=[jax.ShapeDtypeStruct((n // tn, m, tn), F32),
                   jax.ShapeDtypeStruct((m, n), BF16)],
        name="proj_kv",
        compiler_params=_cparams(("parallel", "parallel")),
    )(x, w)


def _logf_body(x_ref, w_ref, b_ref, o_ref):
    z = jnp.dot(x_ref[...].astype(BF16), w_ref[...], preferred_element_type=F32) + b_ref[...]
    o_ref[...] = jnp.minimum(z, 0.0) - jnp.log1p(jnp.exp(-jnp.abs(z)))


def _logf_proj(x, w_pad, b_pad, tm=512):
    m, k = x.shape
    tm = min(tm, m)
    return pl.pallas_call(
        _logf_body,
        grid=(m // tm,),
        in_specs=[pl.BlockSpec((tm, k), lambda i: (i, 0)),
                  pl.BlockSpec((k, LANES), lambda i: (0, 0)),
                  pl.BlockSpec((1, LANES), lambda i: (0, 0))],
        out_specs=pl.BlockSpec((tm, LANES), lambda i: (i, 0)),
        out_shape=jax.ShapeDtypeStruct((m, LANES), F32),
        name="proj_logf",
        compiler_params=_cparams(("parallel",)),
    )(x, w_pad, b_pad)


def _dot01(a, b01):
    hi = a.astype(BF16)
    r1 = a - hi.astype(F32)
    mid = r1.astype(BF16)
    lo = (r1 - mid.astype(F32)).astype(BF16)
    d = functools.partial(jnp.dot, preferred_element_type=F32)
    return d(hi, b01) + d(mid, b01) + d(lo, b01)


def _upper_ones(n):
    r = lax.broadcasted_iota(jnp.int32, (n, n), 0)
    c = lax.broadcasted_iota(jnp.int32, (n, n), 1)
    return jnp.where(r <= c, 1.0, 0.0).astype(BF16)


def _cumsum_body(x_ref, o_ref):
    x = x_ref[0]
    rows = x.shape[0]
    within = _dot01(x, _upper_ones(LANES))
    r = lax.broadcasted_iota(jnp.int32, (rows, rows), 0)
    c = lax.broadcasted_iota(jnp.int32, (rows, rows), 1)
    strict_lower = jnp.where(c < r, 1.0, 0.0).astype(BF16)
    x_hi = x.astype(BF16)
    r1 = x - x_hi.astype(F32)
    x_mid = r1.astype(BF16)
    x_lo = (r1 - x_mid.astype(F32)).astype(BF16)
    d = functools.partial(jnp.dot, preferred_element_type=F32)
    prev = d(strict_lower, x_hi) + d(strict_lower, x_mid) + d(strict_lower, x_lo)
    all_ones = jnp.ones((LANES, LANES), BF16)
    o_ref[0] = within + _dot01(prev, all_ones)


def _cumsum_seq(x):
    n, t = x.shape
    rows = t // LANES
    x3 = x.reshape(n, rows, LANES)
    out = pl.pallas_call(
        _cumsum_body,
        grid=(n,),
        in_specs=[pl.BlockSpec((1, rows, LANES), lambda i: (i, 0, 0))],
        out_specs=pl.BlockSpec((1, rows, LANES), lambda i: (i, 0, 0)),
        out_shape=jax.ShapeDtypeStruct((n, rows, LANES), F32),
        name="logf_cumsum",
        compiler_params=_cparams(("parallel",)),
    )(x3)
    return out.reshape(n, t)


def _nt_dot(a, b):
    return lax.dot_general(a, b, (((1,), (1,)), ((), ())), preferred_element_type=F32)


def _softmax_step(s, m_ref, acc_ref, v_chunk):
    m_old = m_ref[...]
    m_new = jnp.maximum(m_old, jnp.broadcast_to(jnp.max(s, axis=1, keepdims=True), m_old.shape))
    alpha = jnp.exp(m_old - m_new)
    p = jnp.exp(s - pltpu.repeat(m_new, s.shape[1] // LANES, axis=1)).astype(BF16)
    n_acc = acc_ref.shape[-1] // LANES
    alpha_w = alpha if n_acc == 1 else pltpu.repeat(alpha, n_acc, axis=1)
    acc_ref[...] = alpha_w * acc_ref[...] + jnp.dot(p, v_chunk, preferred_element_type=F32)
    m_ref[...] = m_new


def _causal_sweep(i, tq, tk, chunk):
    per_q = tq // tk

    def full_chunk(j, carry):
        chunk(j, None)
        return carry

    lax.fori_loop(0, i * per_q, full_chunk, 0)
    for r in range(per_q):
        chunk(i * per_q + r, r * tk)


def _causal_mask(s, col_off):
    row = lax.broadcasted_iota(jnp.int32, s.shape, 0)
    col = lax.broadcasted_iota(jnp.int32, s.shape, 1) + col_off
    return jnp.where(col <= row, s, NEG)


def _fox_body(q_ref, k_ref, v_ref, f_ref, o_ref, vaug_ref, acc_ref, m_ref, *, tq, tk):
    i = pl.program_id(2)
    t_all = k_ref.shape[1]
    half = D_FOX

    @pl.when(i == 0)
    def _():
        for c in range(t_all // tk):
            sl = slice(c * tk, (c + 1) * tk)
            v = v_ref[0, sl, :].astype(F32)
            lane = lax.broadcasted_iota(jnp.int32, v.shape, 1)
            v0 = jnp.where(lane < half, v, jnp.where(lane == half, 1.0, 0.0))
            v1 = jnp.where(lane >= half, v, jnp.where(lane == 0, 1.0, 0.0))
            vaug_ref[0, sl, :] = v0.astype(BF16)
            vaug_ref[1, sl, :] = v1.astype(BF16)

    q = q_ref[0]
    qlane = lax.broadcasted_iota(jnp.int32, q.shape, 1)
    q_heads = [jnp.where((qlane >= half) == (hh == 1), q, jnp.zeros_like(q)) for hh in range(2)]
    m_ref[...] = jnp.full(m_ref.shape, NEG, F32)
    acc_ref[...] = jnp.zeros(acc_ref.shape, F32)

    def chunk(j, mask_off):
        start = pl.multiple_of(j * tk, tk)
        kc = k_ref[0, pl.ds(start, tk), :]
        for hh in range(2):
            s = _nt_dot(q_heads[hh], kc) - f_ref[0, hh, pl.ds(j, 1), :]
            if mask_off is not None:
                s = _causal_mask(s, mask_off)
            _softmax_step(s, m_ref.at[hh], acc_ref.at[hh], vaug_ref[hh, pl.ds(start, tk), :])

    _causal_sweep(i, tq, tk, chunk)
    acc0 = acc_ref[0]
    acc1 = acc_ref[1]
    o0 = acc0 * (1.0 / acc0[:, half:half + 1])
    o1 = acc1 * (1.0 / acc1[:, 0:1])
    lane = lax.broadcasted_iota(jnp.int32, o0.shape, 1)
    o_ref[0] = jnp.where(lane < half, o0, o1).astype(o_ref.dtype)


def _fox_attention(qa, kva, fcum, b, t, q_col0, k_col0, v_col0, tq, tk):
    npair = H_FOX // 2
    return pl.pallas_call(
        functools.partial(_fox_body, tq=tq, tk=tk),
        grid=(b, npair, t // tq),
        in_specs=[pl.BlockSpec((1, tq, LANES), lambda bi, g, i: (bi, i, q_col0 + g)),
                  pl.BlockSpec((1, t, LANES), lambda bi, g, i: (bi, 0, k_col0 + g)),
                  pl.BlockSpec((1, t, LANES), lambda bi, g, i: (bi, 0, v_col0 + g)),
                  pl.BlockSpec((1, 2, t // tk, tk), lambda bi, g, i: (bi, g, 0, 0))],
        out_specs=pl.BlockSpec((1, tq, LANES), lambda bi, g, i: (bi, i, g)),
        out_shape=jax.ShapeDtypeStruct((b, t, W_BR), BF16),
        scratch_shapes=[pltpu.VMEM((2, t, LANES), BF16),
                        pltpu.VMEM((2, tq, LANES), F32),
                        pltpu.VMEM((2, tq, LANES), F32)],
        name="fox_attn",
        compiler_params=_cparams(("parallel", "parallel", "arbitrary")),
    )(qa, kva, kva, fcum)


def _lambda_value(lam_ref, lam_init):
    lv = lam_ref[...]
    a = jnp.sum(lv[0:1] * lv[1:2], axis=1, keepdims=True)
    c = jnp.sum(lv[2:3] * lv[3:4], axis=1, keepdims=True)
    return jnp.exp(a) - jnp.exp(c) + lam_init


def _rms_scale(o, g, lam_init):
    ms = jnp.mean(o * o, axis=-1, keepdims=True)
    return o * lax.rsqrt(ms + RMS_EPS) * g * (1.0 - lam_init)


def _diff_body(slope_ref, q_ref, k_ref, v_ref, lam_ref, g_ref, o_ref,
               vaug_ref, acc_ref, m_ref, *, tq, tk, lam_init):
    h = pl.program_id(1)
    i = pl.program_id(2)
    t_all = k_ref.shape[1]
    half = D_DIFF
    dv = 2 * D_DIFF

    @pl.when(i == 0)
    def _():
        for c in range(t_all // tk):
            sl = slice(c * tk, (c + 1) * tk)
            vaug_ref[sl, :dv] = v_ref[0, sl, :]
            lane = lax.broadcasted_iota(jnp.int32, (tk, LANES), 1)
            vaug_ref[sl, dv:] = jnp.where(lane == 0, 1.0, 0.0).astype(BF16)

    slope = slope_ref[h]
    q = q_ref[0]
    qlane = lax.broadcasted_iota(jnp.int32, q.shape, 1)
    q_maps = [jnp.where((qlane >= half) == (mm == 1), q, jnp.zeros_like(q)) for mm in range(2)]
    m_ref[...] = jnp.full(m_ref.shape, NEG, F32)
    acc_ref[...] = jnp.zeros(acc_ref.shape, F32)

    def chunk(j, mask_off):
        start = pl.multiple_of(j * tk, tk)
        kc = k_ref[0, pl.ds(start, tk), :]
        vc = vaug_ref[pl.ds(start, tk), :]
        rel = lax.broadcasted_iota(jnp.int32, (1, tk), 1) + (j * tk - i * tq)
        bias = slope * rel.astype(F32)
        for mm in range(2):
            s = _nt_dot(q_maps[mm], kc) + bias
            if mask_off is not None:
                s = _causal_mask(s, mask_off)
            _softmax_step(s, m_ref.at[mm], acc_ref.at[mm], vc)

    _causal_sweep(i, tq, tk, chunk)
    a0 = acc_ref[0]
    a1 = acc_ref[1]
    o0 = a0[:, :dv] * (1.0 / a0[:, dv:dv + 1])
    o1 = a1[:, :dv] * (1.0 / a1[:, dv:dv + 1])
    lam = _lambda_value(lam_ref, lam_init)
    o_ref[0] = _rms_scale(o0 - lam * o1, g_ref[...], lam_init).astype(o_ref.dtype)


def _diff_attention(qa, kva, slopes, lamv, g, b, t, q_col0, k_col0, v_col0, lam_init, tq, tk):
    dv = 2 * D_DIFF
    return pl.pallas_call(
        functools.partial(_diff_body, tq=tq, tk=tk, lam_init=lam_init),
        grid=(b, H_DIFF, t // tq),
        in_specs=[pl.BlockSpec(memory_space=pltpu.SMEM),
                  pl.BlockSpec((1, tq, LANES), lambda bi, h, i: (bi, i, q_col0 + h)),
                  pl.BlockSpec((1, t, LANES), lambda bi, h, i: (bi, 0, k_col0 + h)),
                  pl.BlockSpec((1, t, LANES), lambda bi, h, i: (bi, 0, v_col0 + h)),
                  pl.BlockSpec((4, D_DIFF), lambda bi, h, i: (0, 0)),
                  pl.BlockSpec((1, dv), lambda bi, h, i: (0, 0))],
        out_specs=pl.BlockSpec((1, tq, LANES), lambda bi, h, i: (bi, i, h)),
        out_shape=jax.ShapeDtypeStruct((b, t, W_BR), BF16),
        scratch_shapes=[pltpu.VMEM((t, dv + LANES), BF16),
                        pltpu.VMEM((2, tq, dv + LANES), F32),
                        pltpu.VMEM((2, tq, LANES), F32)],
        name="diff_attn",
        compiler_params=_cparams(("parallel", "parallel", "arbitrary")),
    )(slopes, qa, kva, kva, lamv, g)


def _mem_body(q_ref, k_ref, v_ref, o_ref):
    scale = D_MEM ** -0.5
    for h in range(H_MEM):
        sl = slice(h * D_MEM, (h + 1) * D_MEM)
        s = _nt_dot(q_ref[0, :, sl], k_ref[0, :, sl]) * scale
        m = jnp.max(s, axis=1, keepdims=True)
        p = jnp.exp(s - m)
        l = jnp.sum(p, axis=1, keepdims=True)
        o = jnp.dot(p.astype(BF16), v_ref[0, :, sl], preferred_element_type=F32)
        o_ref[0, :, sl] = (o * (1.0 / l)).astype(o_ref.dtype)


def _mem_attention(qa, mkv, b, t, q_col0, tq=512):
    n_mem = mkv.shape[1]
    return pl.pallas_call(
        _mem_body,
        grid=(b, t // tq),
        in_specs=[pl.BlockSpec((1, tq, W_BR), lambda bi, i: (bi, i, q_col0)),
                  pl.BlockSpec((1, n_mem, W_BR), lambda bi, i: (bi, 0, 0)),
                  pl.BlockSpec((1, n_mem, W_BR), lambda bi, i: (bi, 0, 1))],
        out_specs=pl.BlockSpec((1, tq, W_BR), lambda bi, i: (bi, i, 0)),
        out_shape=jax.ShapeDtypeStruct((b, t, W_BR), BF16),
        name="mem_attn",
        compiler_params=_cparams(("parallel", "parallel")),
    )(qa, mkv, mkv)


def _silu_gate(o_ref, g_ref):
    g = g_ref[...].astype(F32)
    return (o_ref[...].astype(F32) * (g * jax.nn.sigmoid(g))).astype(BF16)


def _merge_body(x_ref, of_ref, od_ref, om_ref, fg_ref, dg_ref, mg_ref, gates_ref,
                wpf_ref, wpd_ref, wpm_ref, wo_ref, lng_ref, lnb_ref, y_ref, *, alpha):
    d = functools.partial(jnp.dot, preferred_element_type=F32)
    merged = None
    for n, (o_ref, g_ref, w_ref) in enumerate(((of_ref, fg_ref, wpf_ref),
                                               (od_ref, dg_ref, wpd_ref),
                                               (om_ref, mg_ref, wpm_ref))):
        gate = jax.nn.sigmoid(gates_ref[:, n * D_MODEL:(n + 1) * D_MODEL].astype(F32))
        term = gate * d(_silu_gate(o_ref, g_ref), w_ref[...])
        merged = term if merged is None else merged + term
    z = alpha * x_ref[...] + d(merged.astype(BF16), wo_ref[...])
    mu = jnp.mean(z, axis=-1, keepdims=True)
    zc = z - mu
    var = jnp.mean(zc * zc, axis=-1, keepdims=True)
    y_ref[...] = zc * lax.rsqrt(var + LN_EPS) * lng_ref[...] + lnb_ref[...]


def _merge(x, o_fox, o_diff, o_mem, proj, cols, wpf, wpd, wpm, wo, ln_g, ln_b, alpha, tm=512):
    m = x.shape[0]
    tm = min(tm, m)
    fg_c, dg_c, mg_c, gates_c = cols
    row = lambda i: (i, 0)
    const = lambda i: (0, 0)
    return pl.pallas_call(
        functools.partial(_merge_body, alpha=alpha),
        grid=(m // tm,),
        in_specs=[pl.BlockSpec((tm, D_MODEL), row),
                  pl.BlockSpec((tm, W_BR), row),
                  pl.BlockSpec((tm, W_BR), row),
                  pl.BlockSpec((tm, W_BR), row),
                  pl.BlockSpec((tm, W_BR), lambda i: (i, fg_c)),
                  pl.BlockSpec((tm, W_BR), lambda i: (i, dg_c)),
                  pl.BlockSpec((tm, W_BR), lambda i: (i, mg_c)),
                  pl.BlockSpec((tm, N_BRANCH * D_MODEL), lambda i: (i, gates_c)),
                  pl.BlockSpec((W_BR, D_MODEL), const),
                  pl.BlockSpec((W_BR, D_MODEL), const),
                  pl.BlockSpec((W_BR, D_MODEL), const),
                  pl.BlockSpec((D_MODEL, D_MODEL), const),
                  pl.BlockSpec((1, D_MODEL), const),
                  pl.BlockSpec((1, D_MODEL), const)],
        out_specs=pl.BlockSpec((tm, D_MODEL), row),
        out_shape=jax.ShapeDtypeStruct((m, D_MODEL), F32),
        name="merge",
        compiler_params=_cparams(("parallel",)),
    )(x, o_fox, o_diff, o_mem, proj, proj, proj, proj, wpf, wpd, wpm, wo, ln_g, ln_b)


PAGES_PER_STEP = 4


def _masked_softmax_step(s, valid, m_ref, l_ref):
    if valid is not None:
        s = jnp.where(valid, s, NEG)
    m_old = m_ref[...]
    m_new = jnp.maximum(m_old, jnp.broadcast_to(jnp.max(s, axis=1, keepdims=True), m_old.shape))
    alpha = jnp.exp(m_old - m_new)
    n_rep = s.shape[1] // LANES
    p = jnp.exp(s - (m_new if n_rep == 1 else pltpu.repeat(m_new, n_rep, axis=1)))
    if valid is not None:
        p = jnp.where(valid, p, 0.0)
    l_ref[...] = alpha * l_ref[...] + jnp.broadcast_to(jnp.sum(p, axis=1, keepdims=True), m_old.shape)
    m_ref[...] = m_new
    return alpha, p


def _sample_body(pt_ref, qf_ref, qd_ref, qm_ref, *refs, n_steps, pp, page, t_new, lam_init):
    del pt_ref
    cfk, cfv, cfl, cdk, cdv = (refs[n * pp:(n + 1) * pp] for n in range(5))
    (nfk_ref, nfv_ref, nfl_ref, ndk_ref, ndv_ref, mk_ref, mv_ref, slope_ref, lam_ref, g_ref,
     of_ref, od_ref, om_ref, mf_ref, lf_ref, af_ref, md_ref, ld_ref, ad_ref, fc_ref) = refs[5 * pp:]
    step_id = pl.program_id(1)
    dv = 2 * D_DIFF
    rows = H_FOX * t_new
    nd = page * H_DIFF
    n_mem = mk_ref.shape[2]

    def head_match(n_rows, n_cols, rows_per_head, n_head):
        r = lax.broadcasted_iota(jnp.int32, (n_rows, n_cols), 0)
        c = lax.broadcasted_iota(jnp.int32, (n_rows, n_cols), 1)
        return (c & (n_head - 1)) == (r >> _log2(rows_per_head)), r, c

    @pl.when(step_id == 0)
    def _():
        mf_ref[...] = jnp.full(mf_ref.shape, NEG, F32)
        md_ref[...] = jnp.full(md_ref.shape, NEG, F32)
        for r in (lf_ref, af_ref, ld_ref, ad_ref, fc_ref):
            r[...] = jnp.zeros(r.shape, F32)
        mk = mk_ref[0, 0].reshape(n_mem * H_MEM, D_MEM).astype(BF16)
        mv = mv_ref[0, 0].reshape(n_mem * H_MEM, D_MEM).astype(BF16)
        valid, _, _ = head_match(H_MEM * t_new, n_mem * H_MEM, t_new, H_MEM)
        s = jnp.where(valid, _nt_dot(qm_ref[0], mk) * (D_MEM ** -0.5), NEG)
        p = jnp.where(valid, jnp.exp(s - jnp.max(s, axis=1, keepdims=True)), 0.0)
        l = jnp.sum(p, axis=1, keepdims=True)
        o = jnp.dot(p.astype(BF16), mv, preferred_element_type=F32) * (1.0 / l)
        for h in range(H_MEM):
            om_ref[0, :, h * D_MEM:(h + 1) * D_MEM] = o[h * t_new:(h + 1) * t_new].astype(om_ref.dtype)

    def page_step(fox_scores, fox_pv, fl, dk, dv_, pos0, new):
        fcum = fc_ref[...] + _dot01(fl, _upper_ones(page))
        fc_ref[...] = jnp.broadcast_to(fcum[:, page - 1:page], fc_ref.shape)
        s = jnp.concatenate([fox_scores(h) - fcum[h:h + 1, :] for h in range(H_FOX)], axis=0)
        valid = None
        if new:
            r = lax.broadcasted_iota(jnp.int32, (rows, page), 0)
            c = lax.broadcasted_iota(jnp.int32, (rows, page), 1)
            valid = c <= (r & (t_new - 1))
        alpha, p = _masked_softmax_step(s, valid, mf_ref, lf_ref)
        pv = jnp.concatenate([fox_pv(h, p[h * t_new:(h + 1) * t_new]) for h in range(H_FOX)], axis=0)
        af_ref[...] = alpha[:, :D_FOX] * af_ref[...] + pv
        valid, r, c = head_match(rows, nd, 2 * t_new, H_DIFF)
        if new:
            valid = valid & ((c >> _log2(H_DIFF)) <= (r & (t_new - 1)))
        c1 = lax.broadcasted_iota(jnp.int32, (1, nd), 1)
        slope_row = jnp.zeros((1, nd), F32)
        for h in range(H_DIFF):
            slope_row = jnp.where((c1 & (H_DIFF - 1)) == h, slope_ref[h], slope_row)
        bias = slope_row * ((c1 >> _log2(H_DIFF)) + pos0).astype(F32)
        s = _nt_dot(qd_ref[0], dk) + bias
        alpha, p = _masked_softmax_step(s, valid, md_ref, ld_ref)
        ad_ref[...] = alpha * ad_ref[...] + jnp.dot(p.astype(BF16), dv_, preferred_element_type=F32)

    for r in range(pp):
        page_step(lambda h, r=r: jnp.dot(qf_ref[0, h], cfk[r][0, 0, h].astype(BF16), preferred_element_type=F32),
                  lambda h, p, r=r: _nt_dot(p.astype(BF16), cfv[r][0, 0, h].astype(BF16)),
                  cfl[r][0, 0],
                  cdk[r][0, 0].reshape(nd, dv).astype(BF16),
                  cdv[r][0, 0].reshape(nd, dv).astype(BF16),
                  (step_id * pp + r - n_steps * pp) * page, False)

    @pl.when(step_id == n_steps - 1)
    def _():
        def pad_rows(x, n):
            return jnp.concatenate([x, jnp.zeros((n - x.shape[0], x.shape[1]), x.dtype)], axis=0).astype(BF16)

        page_step(lambda h: _nt_dot(qf_ref[0, h], pad_rows(nfk_ref[0, h], page)),
                  lambda h, p: jnp.dot(p.astype(BF16), pad_rows(nfv_ref[0, h], page), preferred_element_type=F32),
                  nfl_ref[0],
                  pad_rows(ndk_ref[0].reshape(t_new * H_DIFF, dv), nd),
                  pad_rows(ndv_ref[0].reshape(t_new * H_DIFF, dv), nd), 0, True)
        o = af_ref[...] * (1.0 / lf_ref[:, :D_FOX])
        of_ref[0] = o.reshape(H_FOX, t_new, D_FOX).astype(of_ref.dtype)
        lam = _lambda_value(lam_ref, lam_init)
        o = ad_ref[...] * (1.0 / ld_ref[...])
        for h in range(H_DIFF):
            o1 = o[h * 2 * t_new:h * 2 * t_new + t_new]
            o2 = o[h * 2 * t_new + t_new:(h + 1) * 2 * t_new]
            od_ref[0, :, h * dv:(h + 1) * dv] = _rms_scale(o1 - lam * o2, g_ref[...], lam_init).astype(od_ref.dtype)


def _sample_attention(l, page_table, qf, qd, qm, caches, news, mem_k, mem_v, slopes, lamv, g, lam_init):
    cfk, cfv, cfl, cdk, cdv = caches
    n_seq, n_pages = page_table.shape
    page = cdk.shape[2]
    t_new = news[0].shape[2]
    n_mem = mem_k.shape[2]
    pp = PAGES_PER_STEP
    n_steps = n_pages // pp
    dv = 2 * D_DIFF

    def cache_spec(shape, r):
        return pl.BlockSpec((1, 1) + shape,
                            lambda s, p, pt, r=r: (l, pt[s * n_pages + p * pp + r]) + (0,) * len(shape))

    def seq_spec(shape):
        return pl.BlockSpec((1,) + shape, lambda s, p, pt: (s,) + (0,) * len(shape))

    const2 = lambda s, p, pt: (0, 0)
    mem_spec = pl.BlockSpec((1, 1, n_mem, H_MEM, D_MEM), lambda s, p, pt: (l, s, 0, 0, 0))
    rows = H_FOX * t_new
    in_specs = [seq_spec((H_FOX, t_new, D_FOX)), seq_spec((rows, dv)), seq_spec((H_MEM * t_new, D_MEM))]
    operands = [qf, qd, qm]
    for arr, shape in ((cfk, (H_FOX, D_FOX, page)), (cfv, (H_FOX, D_FOX, page)), (cfl, (H_FOX, page)),
                       (cdk, (page, H_DIFF, dv)), (cdv, (page, H_DIFF, dv))):
        for r in range(pp):
            in_specs.append(cache_spec(shape, r))
            operands.append(arr)
    in_specs += [seq_spec((H_FOX, t_new, D_FOX)), seq_spec((H_FOX, t_new, D_FOX)), seq_spec((H_FOX, page)),
                 seq_spec((t_new, H_DIFF, dv)), seq_spec((t_new, H_DIFF, dv)),
                 mem_spec, mem_spec,
                 pl.BlockSpec(memory_space=pltpu.SMEM),
                 pl.BlockSpec((4, D_DIFF), const2), pl.BlockSpec((1, dv), const2)]
    operands += list(news) + [mem_k, mem_v, slopes, lamv, g]
    grid_spec = pltpu.PrefetchScalarGridSpec(
        num_scalar_prefetch=1,
        grid=(n_seq, n_steps),
        in_specs=in_specs,
        out_specs=[seq_spec((H_FOX, t_new, D_FOX)), seq_spec((t_new, W_BR)), seq_spec((t_new, W_BR))],
        scratch_shapes=[pltpu.VMEM((rows, LANES), F32), pltpu.VMEM((rows, LANES), F32),
                        pltpu.VMEM((rows, D_FOX), F32),
                        pltpu.VMEM((rows, LANES), F32), pltpu.VMEM((rows, LANES), F32),
                        pltpu.VMEM((rows, dv), F32),
                        pltpu.VMEM((8, LANES), F32)])
    return pl.pallas_call(
        functools.partial(_sample_body, n_steps=n_steps, pp=pp, page=page, t_new=t_new, lam_init=lam_init),
        grid_spec=grid_spec,
        out_shape=[jax.ShapeDtypeStruct((n_seq, H_FOX, t_new, D_FOX), BF16),
                   jax.ShapeDtypeStruct((n_seq, t_new, W_BR), BF16),
                   jax.ShapeDtypeStruct((n_seq, t_new, W_BR), BF16)],
        name="sample_attn",
        compiler_params=_cparams(("parallel", "arbitrary")),
    )(page_table.reshape(-1), *operands)


_A_FQ, _A_FG, _A_DQ, _A_DG, _A_MQ, _A_MG = range(6)
_A_GATES = 6 * W_BR
_KV_FK, _KV_FV, _KV_DK, _KV_DV = range(4)


def _split_w_in(w):
    sizes = (W_BR, W_BR, W_BR, W_BR, H_FOX, W_BR, W_BR, W_BR, W_BR, W_BR, W_BR, N_BRANCH * D_MODEL)
    offs = [0]
    for s in sizes:
        offs.append(offs[-1] + s)
    fq, fk, fv, fg, ff, dq, dk, dv, dg, mq, mg, gates = [w[:, offs[n]:offs[n + 1]] for n in range(12)]
    w_act = jnp.concatenate([fq * (D_FOX ** -0.5), fg, dq * (D_DIFF ** -0.5), dg, mq, mg, gates],
                            axis=1).astype(BF16)
    w_kv = jnp.concatenate([fk, fv, dk, dv], axis=1).astype(BF16)
    w_ff = jnp.pad(ff, ((0, 0), (0, LANES - H_FOX))).astype(BF16)
    return w_act, w_kv, w_ff


def kernel(x_prompt, x_sample, mem_prompt, cache_fox_k, cache_fox_v, cache_fox_logf, cache_diff_k,
           cache_diff_v, cache_mem_k, cache_mem_v, page_table, w_in, b_forget, w_mem_kv, lambda_q1,
           lambda_k1, lambda_q2, lambda_k2, diff_norm_g, w_proj_fox, w_proj_diff, w_proj_mem, w_out,
           ln_g, ln_b):
    depth = w_in.shape[0]
    b, t, _ = x_prompt.shape
    n_seq, t_new, _ = x_sample.shape
    page = cache_fox_k.shape[2]
    n_mem = mem_prompt.shape[1]
    alpha = (2 * depth) ** 0.25
    tq = min(1024, t)
    tk = min(512, t)
    dv = 2 * D_DIFF
    nblk = W_BR // LANES

    slopes = 2.0 ** (-8.0 * jnp.arange(1, H_DIFF + 1, dtype=F32) / H_DIFF)
    cfk = jnp.transpose(cache_fox_k, (0, 1, 3, 4, 2))
    cfv = jnp.transpose(cache_fox_v, (0, 1, 3, 4, 2))
    cfl = jnp.swapaxes(cache_fox_logf, 2, 3)
    caches = (cfk, cfv, cfl, cache_diff_k, cache_diff_v)
    lane_lo = (jnp.arange(dv) < D_DIFF)

    xp = x_prompt.reshape(b * t, D_MODEL)
    xs = x_sample.reshape(n_seq * t_new, D_MODEL)
    m_s = n_seq * t_new
    mem2 = mem_prompt.reshape(b * n_mem, D_MODEL)
    outs = [[] for _ in range(12)]

    for l in range(depth):
        lam_init = 0.8 - 0.6 * math.exp(-0.3 * l)
        w_act, w_kv, w_ff = _split_w_in(w_in[l])
        b_ff = jnp.pad(b_forget[l], (0, LANES - H_FOX)).reshape(1, LANES)
        lamv = jnp.stack([lambda_q1[l], lambda_k1[l], lambda_q2[l], lambda_k2[l]])
        g = diff_norm_g[l].reshape(1, dv)
        merge_w = (w_proj_fox[l].astype(BF16), w_proj_diff[l].astype(BF16), w_proj_mem[l].astype(BF16),
                   w_out[l].astype(BF16), ln_g[l].reshape(1, D_MODEL), ln_b[l].reshape(1, D_MODEL))
        merge_cols = (_A_FG, _A_DG, _A_MG, _A_GATES // (N_BRANCH * D_MODEL))

        act = _matmul(xp, w_act, 1536, BF16)
        kv32, kv16 = _matmul_dual(xp, w_kv, W_BR)
        logf = _logf_proj(xp, w_ff, b_ff)[:, :H_FOX]
        mkv32, mkv16 = _matmul_dual(mem2, w_mem_kv[l].astype(BF16), W_BR)
        fcum = _cumsum_seq(jnp.swapaxes(logf.reshape(b, t, H_FOX), 1, 2).reshape(b * H_FOX, t))
        act3 = act.reshape(b, t, -1)
        kv3 = kv16.reshape(b, t, -1)
        o_fox = _fox_attention(act3, kv3, fcum.reshape(b, H_FOX, t // tk, tk), b, t,
                               _A_FQ * nblk, _KV_FK * nblk, _KV_FV * nblk, tq, tk)
        o_diff = _diff_attention(act3, kv3, slopes, lamv, g, b, t,
                                 _A_DQ * nblk, _KV_DK * nblk, _KV_DV * nblk, lam_init, tq, tk)
        o_mem = _mem_attention(act3, mkv16.reshape(b, n_mem, 2 * W_BR), b, t, _A_MQ, tq=tk)
        xp = _merge(xp, o_fox.reshape(b * t, W_BR), o_diff.reshape(b * t, W_BR),
                    o_mem.reshape(b * t, W_BR), act, merge_cols, *merge_w, alpha)
        outs[0].append(kv32[_KV_FK].reshape(b, t, H_FOX, D_FOX))
        outs[1].append(kv32[_KV_FV].reshape(b, t, H_FOX, D_FOX))
        outs[2].append(logf.reshape(b, t, H_FOX))
        outs[3].append(kv32[_KV_DK].reshape(b, t, H_DIFF, dv))
        outs[4].append(kv32[_KV_DV].reshape(b, t, H_DIFF, dv))
        outs[5].append(mkv32[0].reshape(b, n_mem, H_MEM, D_MEM))
        outs[6].append(mkv32[1].reshape(b, n_mem, H_MEM, D_MEM))

        act = _matmul(xs, w_act, 1536, BF16)
        kv32 = _matmul(xs, w_kv, W_BR, F32).reshape(m_s, 4, W_BR)
        logf = _logf_proj(xs, w_ff, b_ff)[:, :H_FOX]
        new_fk = kv32[:, _KV_FK].reshape(n_seq, t_new, H_FOX, D_FOX)
        new_fv = kv32[:, _KV_FV].reshape(n_seq, t_new, H_FOX, D_FOX)
        new_dk = kv32[:, _KV_DK].reshape(n_seq, t_new, H_DIFF, dv)
        new_dv = kv32[:, _KV_DV].reshape(n_seq, t_new, H_DIFF, dv)
        new_fl = logf.reshape(n_seq, t_new, H_FOX)
        act4 = act.reshape(n_seq, t_new, -1)
        qf = jnp.swapaxes(act4[..., _A_FQ * W_BR:(_A_FQ + 1) * W_BR].reshape(n_seq, t_new, H_FOX, D_FOX), 1, 2)
        dq = act4[..., _A_DQ * W_BR:(_A_DQ + 1) * W_BR].reshape(n_seq, t_new, H_DIFF, dv)
        dq2 = jnp.stack([jnp.where(lane_lo, dq, 0), jnp.where(lane_lo, 0, dq)], axis=0)
        qd = jnp.transpose(dq2, (1, 3, 0, 2, 4)).reshape(n_seq, H_DIFF * 2 * t_new, dv)
        qm = jnp.swapaxes(act4[..., _A_MQ * W_BR:(_A_MQ + 1) * W_BR].reshape(n_seq, t_new, H_MEM, D_MEM), 1, 2)
        qm = qm.reshape(n_seq, H_MEM * t_new, D_MEM)
        nfl = jnp.pad(jnp.swapaxes(new_fl, 1, 2), ((0, 0), (0, 0), (0, page - t_new)))
        o_fox, o_diff, o_mem = _sample_attention(
            l, page_table, qf, qd, qm, caches,
            (jnp.swapaxes(new_fk, 1, 2), jnp.swapaxes(new_fv, 1, 2), nfl, new_dk, new_dv),
            cache_mem_k, cache_mem_v, slopes, lamv, g, lam_init)
        o_fox = jnp.swapaxes(o_fox, 1, 2).reshape(m_s, W_BR)
        xs = _merge(xs, o_fox, o_diff.reshape(m_s, W_BR), o_mem.reshape(m_s, W_BR),
                    act, merge_cols, *merge_w, alpha)
        outs[7].append(new_fk)
        outs[8].append(new_fv)
        outs[9].append(new_fl)
        outs[10].append(new_dk)
        outs[11].append(new_dv)

    return (xp.reshape(b, t, D_MODEL), xs.reshape(n_seq, t_new, D_MODEL)) + tuple(
        jnp.stack(o) for o in outs)
```

```python
import functools
import math

import jax
import jax.numpy as jnp
from jax import lax
from jax.experimental import pallas as pl
from jax.experimental.pallas import tpu as pltpu

F32 = jnp.float32
BF16 = jnp.bfloat16

D_MODEL = 1024
H_FOX, D_FOX = 8, 64
H_DIFF, D_DIFF = 4, 64
H_MEM, D_MEM = 4, 128
W_BR = 512
N_BRANCH = 3
LN_EPS = 1e-5
RMS_EPS = 1e-5
LANES = 128
NEG = -1e30

VMEM_LIMIT = 48 * 1024 * 1024


def _log2(n):
    assert n & (n - 1) == 0, n
    return n.bit_length() - 1


def _cparams(sem):
    return pltpu.CompilerParams(dimension_semantics=sem, vmem_limit_bytes=VMEM_LIMIT)


def _mm_body(x_ref, w_ref, *o_refs):
    r = jnp.dot(x_ref[...].astype(BF16), w_ref[0], preferred_element_type=F32)
    for o in o_refs:
        o[...] = r.astype(o.dtype).reshape(o.shape)


def _matmul(x, w, l, tn, out_dtype, tm=512):
    m, k = x.shape
    n = w.shape[2]
    tm = min(tm, m)
    return pl.pallas_call(
        _mm_body,
        grid=(n // tn, m // tm),
        in_specs=[pl.BlockSpec((tm, k), lambda j, i: (i, 0)),
                  pl.BlockSpec((1, k, tn), lambda j, i: (l, 0, j))],
        out_specs=pl.BlockSpec((tm, tn), lambda j, i: (i, j)),
        out_shape=jax.ShapeDtypeStruct((m, n), out_dtype),
        name="proj",
        compiler_params=_cparams(("parallel", "parallel")),
    )(x, w)


def _matmul_dual(x, w, l, tn, tm=512):
    m, k = x.shape
    n = w.shape[2]
    tm = min(tm, m)
    return pl.pallas_call(
        _mm_body,
        grid=(n // tn, m // tm),
        in_specs=[pl.BlockSpec((tm, k), lambda j, i: (i, 0)),
                  pl.BlockSpec((1, k, tn), lambda j, i: (l, 0, j))],
        out_specs=[pl.BlockSpec((1, tm, tn), lambda j, i: (j, i, 0)),
                   pl.BlockSpec((tm, tn), lambda j, i: (i, j))],
        out_shape=[jax.ShapeDtypeStruct((n // tn, m, tn), F32),
                   jax.ShapeDtypeStruct((m, n), BF16)],
        name="proj_mem_kv",
        compiler_params=_cparams(("parallel", "parallel")),
    )(x, w)


def _kv_body(x_ref, w_ref, *refs, n_out):
    o_refs = refs[-(n_out + 1):]
    r = jnp.dot(x_ref[...].astype(BF16), w_ref[0], preferred_element_type=F32)
    for n in range(n_out):
        o_refs[n][0] = r[:, n * W_BR:(n + 1) * W_BR]
    o_refs[n_out][...] = r.astype(BF16)


def _proj_kv(x, w, l, stacks, tm=512):
    m, k = x.shape
    depth, _, n = w.shape
    n_out = n // W_BR
    tm = min(tm, m)
    in_specs = [pl.BlockSpec((tm, k), lambda i: (i, 0)),
                pl.BlockSpec((1, k, n), lambda i: (l, 0, 0))]
    operands = [x, w]
    aliases = {}
    if stacks is not None:
        in_specs += [pl.BlockSpec(memory_space=pl.ANY)] * n_out
        operands += list(stacks)
        aliases = {2 + i: i for i in range(n_out)}
    res = pl.pallas_call(
        functools.partial(_kv_body, n_out=n_out),
        grid=(m // tm,),
        in_specs=in_specs,
        out_specs=[pl.BlockSpec((1, tm, W_BR), lambda i: (l, i, 0))] * n_out
                  + [pl.BlockSpec((tm, n), lambda i: (i, 0))],
        out_shape=[jax.ShapeDtypeStruct((depth, m, W_BR), F32)] * n_out
                  + [jax.ShapeDtypeStruct((m, n), BF16)],
        input_output_aliases=aliases,
        name="proj_kv",
        compiler_params=_cparams(("parallel",)),
    )(*operands)
    return tuple(res[:n_out]), res[n_out]


def _logf_body(x_ref, w_ref, b_ref, o_ref):
    z = jnp.dot(x_ref[...].astype(BF16), w_ref[0], preferred_element_type=F32) + b_ref[0]
    o_ref[...] = jnp.minimum(z, 0.0) - jnp.log1p(jnp.exp(-jnp.abs(z)))


def _logf_proj(x, w_pad, b_pad, l, tm=512):
    m, k = x.shape
    tm = min(tm, m)
    return pl.pallas_call(
        _logf_body,
        grid=(m // tm,),
        in_specs=[pl.BlockSpec((tm, k), lambda i: (i, 0)),
                  pl.BlockSpec((1, k, LANES), lambda i: (l, 0, 0)),
                  pl.BlockSpec((1, 1, LANES), lambda i: (l, 0, 0))],
        out_specs=pl.BlockSpec((tm, LANES), lambda i: (i, 0)),
        out_shape=jax.ShapeDtypeStruct((m, LANES), F32),
        name="proj_logf",
        compiler_params=_cparams(("parallel",)),
    )(x, w_pad, b_pad)


def _dot01(a, b01):
    hi = a.astype(BF16)
    r1 = a - hi.astype(F32)
    mid = r1.astype(BF16)
    lo = (r1 - mid.astype(F32)).astype(BF16)
    d = functools.partial(jnp.dot, preferred_element_type=F32)
    return d(hi, b01) + d(mid, b01) + d(lo, b01)


def _upper_ones(n):
    r = lax.broadcasted_iota(jnp.int32, (n, n), 0)
    c = lax.broadcasted_iota(jnp.int32, (n, n), 1)
    return jnp.where(r <= c, 1.0, 0.0).astype(BF16)


def _cumsum_body(x_ref, o_ref):
    x = x_ref[0]
    rows = x.shape[0]
    within = _dot01(x, _upper_ones(LANES))
    r = lax.broadcasted_iota(jnp.int32, (rows, rows), 0)
    c = lax.broadcasted_iota(jnp.int32, (rows, rows), 1)
    strict_lower = jnp.where(c < r, 1.0, 0.0).astype(BF16)
    x_hi = x.astype(BF16)
    r1 = x - x_hi.astype(F32)
    x_mid = r1.astype(BF16)
    x_lo = (r1 - x_mid.astype(F32)).astype(BF16)
    d = functools.partial(jnp.dot, preferred_element_type=F32)
    prev = d(strict_lower, x_hi) + d(strict_lower, x_mid) + d(strict_lower, x_lo)
    all_ones = jnp.ones((LANES, LANES), BF16)
    o_ref[0] = within + _dot01(prev, all_ones)


def _cumsum_seq(x):
    n, t = x.shape
    rows = t // LANES
    x3 = x.reshape(n, rows, LANES)
    out = pl.pallas_call(
        _cumsum_body,
        grid=(n,),
        in_specs=[pl.BlockSpec((1, rows, LANES), lambda i: (i, 0, 0))],
        out_specs=pl.BlockSpec((1, rows, LANES), lambda i: (i, 0, 0)),
        out_shape=jax.ShapeDtypeStruct((n, rows, LANES), F32),
        name="logf_cumsum",
        compiler_params=_cparams(("parallel",)),
    )(x3)
    return out.reshape(n, t)


def _nt_dot(a, b):
    return lax.dot_general(a, b, (((1,), (1,)), ((), ())), preferred_element_type=F32)


def _tile_lanes(x, n):
    return x if n == 1 else jnp.concatenate([x] * n, axis=1)


def _softmax_step(s, m_ref, acc_ref, v_chunk):
    m_old = m_ref[...]
    m_new = jnp.maximum(m_old, jnp.broadcast_to(jnp.max(s, axis=1, keepdims=True), m_old.shape))
    alpha = jnp.exp(m_old - m_new)
    p = jnp.exp(s - _tile_lanes(m_new, s.shape[1] // LANES)).astype(BF16)
    acc_ref[...] = (_tile_lanes(alpha, acc_ref.shape[-1] // LANES) * acc_ref[...]
                    + jnp.dot(p, v_chunk, preferred_element_type=F32))
    m_ref[...] = m_new


def _causal_sweep(i, tq, tk, n_streams, scores, consume, s_ref):
    per_q = tq // tk
    assert per_q % 2 == 0, "chunk c lives in half c % 2 of s_ref, kept static by pairing chunks"

    def stage(c, slot):
        for st in range(n_streams):
            s_ref[slot, st] = scores(c, st)

    def chunk_pair(jj, carry):
        c = 2 * jj
        stage(c + 1, 1)
        for st in range(n_streams):
            consume(c, st, s_ref[0, st], None)
        stage(c + 2, 0)
        for st in range(n_streams):
            consume(c + 1, st, s_ref[1, st], None)
        return carry

    stage(0, 0)
    lax.fori_loop(0, i * (per_q // 2), chunk_pair, 0)
    for r in range(per_q):
        c = i * per_q + r
        if r + 1 < per_q:
            stage(c + 1, (r + 1) % 2)
        for st in range(n_streams):
            consume(c, st, s_ref[r % 2, st, pl.ds(r * tk, tq - r * tk), :], r * tk)


def _causal_mask(s):
    row = lax.broadcasted_iota(jnp.int32, s.shape, 0)
    col = lax.broadcasted_iota(jnp.int32, s.shape, 1)
    return jnp.where(col <= row, s, NEG)


def _rows_from(ref, lead, row0):
    if row0 is None:
        return ref.at[lead]
    return ref.at[lead, pl.ds(row0, ref.shape[1] - row0)]


def _fox_body(q_ref, k_ref, v_ref, f_ref, o_ref, vaug_ref, acc_ref, m_ref, s_ref, *, tq, tk):
    i = pl.program_id(2)
    t_all = k_ref.shape[1]
    half = D_FOX

    @pl.when(i == 0)
    def _():
        for c in range(t_all // tk):
            sl = slice(c * tk, (c + 1) * tk)
            v = v_ref[0, sl, :].astype(F32)
            lane = lax.broadcasted_iota(jnp.int32, v.shape, 1)
            v0 = jnp.where(lane < half, v, jnp.where(lane == half, 1.0, 0.0))
            v1 = jnp.where(lane >= half, v, jnp.where(lane == 0, 1.0, 0.0))
            vaug_ref[0, sl, :] = v0.astype(BF16)
            vaug_ref[1, sl, :] = v1.astype(BF16)

    q = q_ref[0]
    qlane = lax.broadcasted_iota(jnp.int32, q.shape, 1)
    q_heads = [jnp.where((qlane >= half) == (hh == 1), q, jnp.zeros_like(q)) for hh in range(2)]
    m_ref[...] = jnp.full(m_ref.shape, NEG, F32)
    acc_ref[...] = jnp.zeros(acc_ref.shape, F32)

    def scores(c, hh):
        start = pl.multiple_of(c * tk, tk)
        return _nt_dot(q_heads[hh], k_ref[0, pl.ds(start, tk), :]) - f_ref[0, hh, pl.ds(c, 1), :]

    def consume(c, hh, s, row0):
        start = pl.multiple_of(c * tk, tk)
        if row0 is not None:
            s = _causal_mask(s)
        _softmax_step(s, _rows_from(m_ref, hh, row0), _rows_from(acc_ref, hh, row0),
                      vaug_ref[hh, pl.ds(start, tk), :])

    _causal_sweep(i, tq, tk, 2, scores, consume, s_ref)
    acc0 = acc_ref[0]
    acc1 = acc_ref[1]
    o0 = acc0 * (1.0 / acc0[:, half:half + 1])
    o1 = acc1 * (1.0 / acc1[:, 0:1])
    lane = lax.broadcasted_iota(jnp.int32, o0.shape, 1)
    o_ref[0] = jnp.where(lane < half, o0, o1).astype(o_ref.dtype)


def _fox_attention(qa, kva, fcum, b, t, q_col0, k_col0, v_col0, tq, tk):
    npair = H_FOX // 2
    return pl.pallas_call(
        functools.partial(_fox_body, tq=tq, tk=tk),
        grid=(b, npair, t // tq),
        in_specs=[pl.BlockSpec((1, tq, LANES), lambda bi, g, i: (bi, i, q_col0 + g)),
                  pl.BlockSpec((1, t, LANES), lambda bi, g, i: (bi, 0, k_col0 + g)),
                  pl.BlockSpec((1, t, LANES), lambda bi, g, i: (bi, 0, v_col0 + g)),
                  pl.BlockSpec((1, 2, t // tk, tk), lambda bi, g, i: (bi, g, 0, 0))],
        out_specs=pl.BlockSpec((1, tq, LANES), lambda bi, g, i: (bi, i, g)),
        out_shape=jax.ShapeDtypeStruct((b, t, W_BR), BF16),
        scratch_shapes=[pltpu.VMEM((2, t, LANES), BF16),
                        pltpu.VMEM((2, tq, LANES), F32),
                        pltpu.VMEM((2, tq, LANES), F32),
                        pltpu.VMEM((2, 2, tq, tk), F32)],
        name="fox_attn",
        compiler_params=_cparams(("parallel", "parallel", "arbitrary")),
    )(qa, kva, kva, fcum)


def _lambda_value(lam_ref, lam_init):
    lv = lam_ref[0]
    a = jnp.sum(lv[0:1] * lv[1:2], axis=1, keepdims=True)
    c = jnp.sum(lv[2:3] * lv[3:4], axis=1, keepdims=True)
    return jnp.exp(a) - jnp.exp(c) + lam_init


def _rms_scale(o, g, lam_init):
    ms = jnp.mean(o * o, axis=-1, keepdims=True)
    return o * lax.rsqrt(ms + RMS_EPS) * g * (1.0 - lam_init)


def _diff_body(slope_ref, q_ref, k_ref, v_ref, lam_ref, g_ref, o_ref,
               vaug_ref, acc_ref, m_ref, s_ref, *, tq, tk, lam_init):
    h = pl.program_id(1)
    i = pl.program_id(2)
    t_all = k_ref.shape[1]
    half = D_DIFF
    dv = 2 * D_DIFF

    @pl.when(i == 0)
    def _():
        for c in range(t_all // tk):
            sl = slice(c * tk, (c + 1) * tk)
            vaug_ref[sl, :dv] = v_ref[0, sl, :]
            lane = lax.broadcasted_iota(jnp.int32, (tk, LANES), 1)
            vaug_ref[sl, dv:] = jnp.where(lane == 0, 1.0, 0.0).astype(BF16)

    slope = slope_ref[h]
    q = q_ref[0]
    qlane = lax.broadcasted_iota(jnp.int32, q.shape, 1)
    q_maps = [jnp.where((qlane >= half) == (mm == 1), q, jnp.zeros_like(q)) for mm in range(2)]
    m_ref[...] = jnp.full(m_ref.shape, NEG, F32)
    acc_ref[...] = jnp.zeros(acc_ref.shape, F32)

    def scores(c, mm):
        start = pl.multiple_of(c * tk, tk)
        rel = lax.broadcasted_iota(jnp.int32, (1, tk), 1) + (c * tk - i * tq)
        return _nt_dot(q_maps[mm], k_ref[0, pl.ds(start, tk), :]) + slope * rel.astype(F32)

    def consume(c, mm, s, row0):
        start = pl.multiple_of(c * tk, tk)
        if row0 is not None:
            s = _causal_mask(s)
        _softmax_step(s, _rows_from(m_ref, mm, row0), _rows_from(acc_ref, mm, row0),
                      vaug_ref[pl.ds(start, tk), :])

    _causal_sweep(i, tq, tk, 2, scores, consume, s_ref)
    a0 = acc_ref[0]
    a1 = acc_ref[1]
    o0 = a0[:, :dv] * (1.0 / a0[:, dv:dv + 1])
    o1 = a1[:, :dv] * (1.0 / a1[:, dv:dv + 1])
    lam = _lambda_value(lam_ref, lam_init)
    o_ref[0] = _rms_scale(o0 - lam * o1, g_ref[0], lam_init).astype(o_ref.dtype)


def _diff_attention(qa, kva, slopes, lamv, g, l, b, t, q_col0, k_col0, v_col0, lam_init, tq, tk):
    dv = 2 * D_DIFF
    return pl.pallas_call(
        functools.partial(_diff_body, tq=tq, tk=tk, lam_init=lam_init),
        grid=(b, H_DIFF, t // tq),
        in_specs=[pl.BlockSpec(memory_space=pltpu.SMEM),
                  pl.BlockSpec((1, tq, LANES), lambda bi, h, i: (bi, i, q_col0 + h)),
                  pl.BlockSpec((1, t, LANES), lambda bi, h, i: (bi, 0, k_col0 + h)),
                  pl.BlockSpec((1, t, LANES), lambda bi, h, i: (bi, 0, v_col0 + h)),
                  pl.BlockSpec((1, 4, D_DIFF), lambda bi, h, i: (l, 0, 0)),
                  pl.BlockSpec((1, 1, dv), lambda bi, h, i: (l, 0, 0))],
        out_specs=pl.BlockSpec((1, tq, LANES), lambda bi, h, i: (bi, i, h)),
        out_shape=jax.ShapeDtypeStruct((b, t, W_BR), BF16),
        scratch_shapes=[pltpu.VMEM((t, dv + LANES), BF16),
                        pltpu.VMEM((2, tq, dv + LANES), F32),
                        pltpu.VMEM((2, tq, LANES), F32),
                        pltpu.VMEM((2, 2, tq, tk), F32)],
        name="diff_attn",
        compiler_params=_cparams(("parallel", "parallel", "arbitrary")),
    )(slopes, qa, kva, kva, lamv, g)


def _mem_body(q_ref, k_ref, v_ref, o_ref):
    scale = D_MEM ** -0.5
    for h in range(H_MEM):
        sl = slice(h * D_MEM, (h + 1) * D_MEM)
        s = _nt_dot(q_ref[0, :, sl], k_ref[0, :, sl]) * scale
        m = jnp.max(s, axis=1, keepdims=True)
        p = jnp.exp(s - m)
        l = jnp.sum(p, axis=1, keepdims=True)
        o = jnp.dot(p.astype(BF16), v_ref[0, :, sl], preferred_element_type=F32)
        o_ref[0, :, sl] = (o * (1.0 / l)).astype(o_ref.dtype)


def _mem_attention(qa, mkv, b, t, q_col0, tq=512):
    n_mem = mkv.shape[1]
    return pl.pallas_call(
        _mem_body,
        grid=(b, t // tq),
        in_specs=[pl.BlockSpec((1, tq, W_BR), lambda bi, i: (bi, i, q_col0)),
                  pl.BlockSpec((1, n_mem, W_BR), lambda bi, i: (bi, 0, 0)),
                  pl.BlockSpec((1, n_mem, W_BR), lambda bi, i: (bi, 0, 1))],
        out_specs=pl.BlockSpec((1, tq, W_BR), lambda bi, i: (bi, i, 0)),
        out_shape=jax.ShapeDtypeStruct((b, t, W_BR), BF16),
        name="mem_attn",
        compiler_params=_cparams(("parallel", "parallel")),
    )(qa, mkv, mkv)


def _silu_gate(o_ref, g_ref):
    g = g_ref[...].astype(F32)
    return (o_ref[...].astype(F32) * (g * jax.nn.sigmoid(g))).astype(BF16)


def _merge_body(x_ref, of_ref, od_ref, om_ref, fg_ref, dg_ref, mg_ref, gates_ref,
                wpf_ref, wpd_ref, wpm_ref, wo_ref, lng_ref, lnb_ref, y_ref, *, alpha):
    d = functools.partial(jnp.dot, preferred_element_type=F32)
    merged = None
    for n, (o_ref, g_ref, w_ref) in enumerate(((of_ref, fg_ref, wpf_ref),
                                               (od_ref, dg_ref, wpd_ref),
                                               (om_ref, mg_ref, wpm_ref))):
        gate = jax.nn.sigmoid(gates_ref[:, n * D_MODEL:(n + 1) * D_MODEL].astype(F32))
        term = gate * d(_silu_gate(o_ref, g_ref), w_ref[0])
        merged = term if merged is None else merged + term
    z = alpha * x_ref[...] + d(merged.astype(BF16), wo_ref[0])
    mu = jnp.mean(z, axis=-1, keepdims=True)
    zc = z - mu
    var = jnp.mean(zc * zc, axis=-1, keepdims=True)
    y_ref[...] = zc * lax.rsqrt(var + LN_EPS) * lng_ref[0] + lnb_ref[0]


def _merge(x, o_fox, o_diff, o_mem, proj, cols, wpf, wpd, wpm, wo, ln_g, ln_b, l, alpha, tm=512):
    m = x.shape[0]
    tm = min(tm, m)
    fg_c, dg_c, mg_c, gates_c = cols
    row = lambda i: (i, 0)
    layer = lambda i: (l, 0, 0)
    return pl.pallas_call(
        functools.partial(_merge_body, alpha=alpha),
        grid=(m // tm,),
        in_specs=[pl.BlockSpec((tm, D_MODEL), row),
                  pl.BlockSpec((tm, W_BR), row),
                  pl.BlockSpec((tm, W_BR), row),
                  pl.BlockSpec((tm, W_BR), row),
                  pl.BlockSpec((tm, W_BR), lambda i: (i, fg_c)),
                  pl.BlockSpec((tm, W_BR), lambda i: (i, dg_c)),
                  pl.BlockSpec((tm, W_BR), lambda i: (i, mg_c)),
                  pl.BlockSpec((tm, N_BRANCH * D_MODEL), lambda i: (i, gates_c)),
                  pl.BlockSpec((1, W_BR, D_MODEL), layer),
                  pl.BlockSpec((1, W_BR, D_MODEL), layer),
                  pl.BlockSpec((1, W_BR, D_MODEL), layer),
                  pl.BlockSpec((1, D_MODEL, D_MODEL), layer),
                  pl.BlockSpec((1, 1, D_MODEL), layer),
                  pl.BlockSpec((1, 1, D_MODEL), layer)],
        out_specs=pl.BlockSpec((tm, D_MODEL), row),
        out_shape=jax.ShapeDtypeStruct((m, D_MODEL), F32),
        name="merge",
        compiler_params=_cparams(("parallel",)),
    )(x, o_fox, o_diff, o_mem, proj, proj, proj, proj, wpf, wpd, wpm, wo, ln_g, ln_b)


PAGES_PER_STEP = 8


def _masked_softmax_step(s, valid, m_ref, l_ref):
    if valid is not None:
        s = jnp.where(valid, s, NEG)
    m_old = m_ref[...]
    m_new = jnp.maximum(m_old, jnp.broadcast_to(jnp.max(s, axis=1, keepdims=True), m_old.shape))
    alpha = jnp.exp(m_old - m_new)
    p = jnp.exp(s - _tile_lanes(m_new, s.shape[1] // LANES))
    if valid is not None:
        p = jnp.where(valid, p, 0.0)
    l_ref[...] = alpha * l_ref[...] + jnp.broadcast_to(jnp.sum(p, axis=1, keepdims=True), m_old.shape)
    m_ref[...] = m_new
    return alpha, p


def _sample_body(pt_ref, qf_ref, qd_ref, qm_ref, *refs, n_steps, pp, page, t_new, lam_init):
    del pt_ref
    cfk, cfv, cfl, cdk, cdv = (refs[n * pp:(n + 1) * pp] for n in range(5))
    (nfk_ref, nfv_ref, nfl_ref, ndk_ref, ndv_ref, mk_ref, mv_ref, slope_ref, lam_ref, g_ref,
     of_ref, od_ref, om_ref, mf_ref, lf_ref, af_ref, md_ref, ld_ref, ad_ref, fc_ref) = refs[5 * pp:]
    step_id = pl.program_id(1)
    dv = 2 * D_DIFF
    rows = H_FOX * t_new
    nd = page * H_DIFF
    n_mem = mk_ref.shape[2]

    def head_match(n_rows, n_cols, rows_per_head, n_head):
        r = lax.broadcasted_iota(jnp.int32, (n_rows, n_cols), 0)
        c = lax.broadcasted_iota(jnp.int32, (n_rows, n_cols), 1)
        return (c & (n_head - 1)) == (r >> _log2(rows_per_head)), r, c

    @pl.when(step_id == 0)
    def _():
        mf_ref[...] = jnp.full(mf_ref.shape, NEG, F32)
        md_ref[...] = jnp.full(md_ref.shape, NEG, F32)
        for r in (lf_ref, af_ref, ld_ref, ad_ref, fc_ref):
            r[...] = jnp.zeros(r.shape, F32)
        mk = mk_ref[0, 0].reshape(n_mem * H_MEM, D_MEM).astype(BF16)
        mv = mv_ref[0, 0].reshape(n_mem * H_MEM, D_MEM).astype(BF16)
        valid, _, _ = head_match(H_MEM * t_new, n_mem * H_MEM, t_new, H_MEM)
        s = jnp.where(valid, _nt_dot(qm_ref[0], mk) * (D_MEM ** -0.5), NEG)
        p = jnp.where(valid, jnp.exp(s - jnp.max(s, axis=1, keepdims=True)), 0.0)
        l = jnp.sum(p, axis=1, keepdims=True)
        o = jnp.dot(p.astype(BF16), mv, preferred_element_type=F32) * (1.0 / l)
        for h in range(H_MEM):
            om_ref[0, :, h * D_MEM:(h + 1) * D_MEM] = o[h * t_new:(h + 1) * t_new].astype(om_ref.dtype)

    def attend(pages, new):
        n_pg = len(pages)
        carry = fc_ref[...]
        fcums = []
        for pg in pages:
            fcums.append(carry + _dot01(pg["fl"], _upper_ones(page)))
            carry = jnp.broadcast_to(fcums[-1][:, page - 1:page], carry.shape)
        fc_ref[...] = carry
        s = jnp.concatenate(
            [jnp.concatenate([pg["fox_scores"](h) - fc[h:h + 1, :] for pg, fc in zip(pages, fcums)], axis=1)
             for h in range(H_FOX)], axis=0)
        valid = None
        if new:
            r = lax.broadcasted_iota(jnp.int32, s.shape, 0)
            c = lax.broadcasted_iota(jnp.int32, s.shape, 1)
            valid = c <= (r & (t_new - 1))
        alpha, p = _masked_softmax_step(s, valid, mf_ref, lf_ref)
        pv = []
        for h in range(H_FOX):
            parts = [pg["fox_pv"](h, p[h * t_new:(h + 1) * t_new, n * page:(n + 1) * page])
                     for n, pg in enumerate(pages)]
            pv.append(functools.reduce(jnp.add, parts))
        af_ref[...] = alpha[:, :D_FOX] * af_ref[...] + jnp.concatenate(pv, axis=0)
        valid, r, c = head_match(rows, n_pg * nd, 2 * t_new, H_DIFF)
        if new:
            valid = valid & ((c >> _log2(H_DIFF)) <= (r & (t_new - 1)))
        c1 = lax.broadcasted_iota(jnp.int32, (1, nd), 1)
        slope_row = jnp.zeros((1, nd), F32)
        for h in range(H_DIFF):
            slope_row = jnp.where((c1 & (H_DIFF - 1)) == h, slope_ref[h], slope_row)
        rel = c1 >> _log2(H_DIFF)
        s = jnp.concatenate([_nt_dot(qd_ref[0], pg["dk"]) + slope_row * (rel + pg["pos0"]).astype(F32)
                             for pg in pages], axis=1)
        alpha, p = _masked_softmax_step(s, valid, md_ref, ld_ref)
        parts = [jnp.dot(p[:, n * nd:(n + 1) * nd].astype(BF16), pg["dv"], preferred_element_type=F32)
                 for n, pg in enumerate(pages)]
        ad_ref[...] = alpha * ad_ref[...] + functools.reduce(jnp.add, parts)

    attend([dict(
        fox_scores=lambda h, r=r: jnp.dot(qf_ref[0, h], cfk[r][0, 0, h].astype(BF16), preferred_element_type=F32),
        fox_pv=lambda h, p, r=r: _nt_dot(p.astype(BF16), cfv[r][0, 0, h].astype(BF16)),
        fl=cfl[r][0, 0],
        dk=cdk[r][0, 0].reshape(nd, dv).astype(BF16),
        dv=cdv[r][0, 0].reshape(nd, dv).astype(BF16),
        pos0=(step_id * pp + r - n_steps * pp) * page) for r in range(pp)], False)

    @pl.when(step_id == n_steps - 1)
    def _():
        def pad_rows(x, n):
            return jnp.concatenate([x, jnp.zeros((n - x.shape[0], x.shape[1]), x.dtype)], axis=0).astype(BF16)

        attend([dict(
            fox_scores=lambda h: _nt_dot(qf_ref[0, h], pad_rows(nfk_ref[0, h], page)),
            fox_pv=lambda h, p: jnp.dot(p.astype(BF16), pad_rows(nfv_ref[0, h], page), preferred_element_type=F32),
            fl=nfl_ref[0],
            dk=pad_rows(ndk_ref[0].reshape(t_new * H_DIFF, dv), nd),
            dv=pad_rows(ndv_ref[0].reshape(t_new * H_DIFF, dv), nd),
            pos0=0)], True)
        o = af_ref[...] * (1.0 / lf_ref[:, :D_FOX])
        of_ref[0] = o.reshape(H_FOX, t_new, D_FOX).astype(of_ref.dtype)
        lam = _lambda_value(lam_ref, lam_init)
        o = ad_ref[...] * (1.0 / ld_ref[...])
        for h in range(H_DIFF):
            o1 = o[h * 2 * t_new:h * 2 * t_new + t_new]
            o2 = o[h * 2 * t_new + t_new:(h + 1) * 2 * t_new]
            od_ref[0, :, h * dv:(h + 1) * dv] = _rms_scale(o1 - lam * o2, g_ref[0], lam_init).astype(od_ref.dtype)


def _sample_attention(l, page_table, qf, qd, qm, caches, news, mem_k, mem_v, slopes, lamv, g, lam_init):
    cfk, cfv, cfl, cdk, cdv = caches
    n_seq, n_pages = page_table.shape
    page = cdk.shape[2]
    t_new = news[0].shape[2]
    n_mem = mem_k.shape[2]
    pp = PAGES_PER_STEP
    assert n_pages % pp == 0, (n_pages, pp)
    n_steps = n_pages // pp
    dv = 2 * D_DIFF

    def cache_spec(shape, r):
        return pl.BlockSpec((1, 1) + shape,
                            lambda s, p, pt, r=r: (l, pt[s * n_pages + p * pp + r]) + (0,) * len(shape))

    def seq_spec(shape):
        return pl.BlockSpec((1,) + shape, lambda s, p, pt: (s,) + (0,) * len(shape))

    layer3 = lambda s, p, pt: (l, 0, 0)
    mem_spec = pl.BlockSpec((1, 1, n_mem, H_MEM, D_MEM), lambda s, p, pt: (l, s, 0, 0, 0))
    rows = H_FOX * t_new
    in_specs = [seq_spec((H_FOX, t_new, D_FOX)), seq_spec((rows, dv)), seq_spec((H_MEM * t_new, D_MEM))]
    operands = [qf, qd, qm]
    for arr, shape in ((cfk, (H_FOX, D_FOX, page)), (cfv, (H_FOX, D_FOX, page)), (cfl, (H_FOX, page)),
                       (cdk, (page, H_DIFF, dv)), (cdv, (page, H_DIFF, dv))):
        for r in range(pp):
            in_specs.append(cache_spec(shape, r))
            operands.append(arr)
    in_specs += [seq_spec((H_FOX, t_new, D_FOX)), seq_spec((H_FOX, t_new, D_FOX)), seq_spec((H_FOX, page)),
                 seq_spec((t_new, H_DIFF, dv)), seq_spec((t_new, H_DIFF, dv)),
                 mem_spec, mem_spec,
                 pl.BlockSpec(memory_space=pltpu.SMEM),
                 pl.BlockSpec((1, 4, D_DIFF), layer3), pl.BlockSpec((1, 1, dv), layer3)]
    operands += list(news) + [mem_k, mem_v, slopes, lamv, g]
    grid_spec = pltpu.PrefetchScalarGridSpec(
        num_scalar_prefetch=1,
        grid=(n_seq, n_steps),
        in_specs=in_specs,
        out_specs=[seq_spec((H_FOX, t_new, D_FOX)), seq_spec((t_new, W_BR)), seq_spec((t_new, W_BR))],
        scratch_shapes=[pltpu.VMEM((rows, LANES), F32), pltpu.VMEM((rows, LANES), F32),
                        pltpu.VMEM((rows, D_FOX), F32),
                        pltpu.VMEM((rows, LANES), F32), pltpu.VMEM((rows, LANES), F32),
                        pltpu.VMEM((rows, dv), F32),
                        pltpu.VMEM((8, LANES), F32)])
    return pl.pallas_call(
        functools.partial(_sample_body, n_steps=n_steps, pp=pp, page=page, t_new=t_new, lam_init=lam_init),
        grid_spec=grid_spec,
        out_shape=[jax.ShapeDtypeStruct((n_seq, H_FOX, t_new, D_FOX), BF16),
                   jax.ShapeDtypeStruct((n_seq, t_new, W_BR), BF16),
                   jax.ShapeDtypeStruct((n_seq, t_new, W_BR), BF16)],
        name="sample_attn",
        compiler_params=_cparams(("parallel", "arbitrary")),
    )(page_table.reshape(-1), *operands)


_A_FQ, _A_FG, _A_DQ, _A_DG, _A_MQ, _A_MG = range(6)
_A_GATES = 6 * W_BR
_KV_FK, _KV_FV, _KV_DK, _KV_DV = range(4)


def _split_w_in(w):
    sizes = (W_BR, W_BR, W_BR, W_BR, H_FOX, W_BR, W_BR, W_BR, W_BR, W_BR, W_BR, N_BRANCH * D_MODEL)
    offs = [0]
    for s in sizes:
        offs.append(offs[-1] + s)
    fq, fk, fv, fg, ff, dq, dk, dv, dg, mq, mg, gates = [w[..., offs[n]:offs[n + 1]] for n in range(12)]
    w_act = jnp.concatenate([fq * (D_FOX ** -0.5), fg, dq * (D_DIFF ** -0.5), dg, mq, mg, gates],
                            axis=-1).astype(BF16)
    w_kv = jnp.concatenate([fk, fv, dk, dv], axis=-1).astype(BF16)
    w_ff = jnp.pad(ff, ((0, 0), (0, 0), (0, LANES - H_FOX))).astype(BF16)
    return w_act, w_kv, w_ff


def kernel(x_prompt, x_sample, mem_prompt, cache_fox_k, cache_fox_v, cache_fox_logf, cache_diff_k,
           cache_diff_v, cache_mem_k, cache_mem_v, page_table, w_in, b_forget, w_mem_kv, lambda_q1,
           lambda_k1, lambda_q2, lambda_k2, diff_norm_g, w_proj_fox, w_proj_diff, w_proj_mem, w_out,
           ln_g, ln_b):
    depth = w_in.shape[0]
    b, t, _ = x_prompt.shape
    n_seq, t_new, _ = x_sample.shape
    page = cache_fox_k.shape[2]
    n_mem = mem_prompt.shape[1]
    alpha = (2 * depth) ** 0.25
    tq = min(1024, t)
    tk = min(512, t)
    dv = 2 * D_DIFF
    nblk = W_BR // LANES

    slopes = 2.0 ** (-8.0 * jnp.arange(1, H_DIFF + 1, dtype=F32) / H_DIFF)
    cfk = jnp.transpose(cache_fox_k, (0, 1, 3, 4, 2))
    cfv = jnp.transpose(cache_fox_v, (0, 1, 3, 4, 2))
    cfl = jnp.swapaxes(cache_fox_logf, 2, 3)
    caches = (cfk, cfv, cfl, cache_diff_k, cache_diff_v)
    lane_lo = (jnp.arange(dv) < D_DIFF)

    w_act, w_kv, w_ff = _split_w_in(w_in)
    w_mkv = w_mem_kv.astype(BF16)
    b_ff = jnp.pad(b_forget, ((0, 0), (0, LANES - H_FOX))).reshape(depth, 1, LANES)
    lamv = jnp.stack([lambda_q1, lambda_k1, lambda_q2, lambda_k2], axis=1)
    g = diff_norm_g.reshape(depth, 1, dv)
    merge_w = (w_proj_fox.astype(BF16), w_proj_diff.astype(BF16), w_proj_mem.astype(BF16),
               w_out.astype(BF16), ln_g.reshape(depth, 1, D_MODEL), ln_b.reshape(depth, 1, D_MODEL))
    merge_cols = (_A_FG, _A_DG, _A_MG, _A_GATES // (N_BRANCH * D_MODEL))

    xp = x_prompt.reshape(b * t, D_MODEL)
    xs = x_sample.reshape(n_seq * t_new, D_MODEL)
    m_s = n_seq * t_new
    mem2 = mem_prompt.reshape(b * n_mem, D_MODEL)
    outs = [[] for _ in range(12)]
    kv_stacks = None

    for l in range(depth):
        lam_init = 0.8 - 0.6 * math.exp(-0.3 * l)

        act = _matmul(xp, w_act, l, 3072, BF16)
        kv_stacks, kv16 = _proj_kv(xp, w_kv, l, kv_stacks)
        logf = _logf_proj(xp, w_ff, b_ff, l)[:, :H_FOX]
        mkv32, mkv16 = _matmul_dual(mem2, w_mkv, l, W_BR)
        fcum = _cumsum_seq(jnp.swapaxes(logf.reshape(b, t, H_FOX), 1, 2).reshape(b * H_FOX, t))
        act3 = act.reshape(b, t, -1)
        kv3 = kv16.reshape(b, t, -1)
        o_fox = _fox_attention(act3, kv3, fcum.reshape(b, H_FOX, t // tk, tk), b, t,
                               _A_FQ * nblk, _KV_FK * nblk, _KV_FV * nblk, tq, tk)
        o_diff = _diff_attention(act3, kv3, slopes, lamv, g, l, b, t,
                                 _A_DQ * nblk, _KV_DK * nblk, _KV_DV * nblk, lam_init, tq, tk)
        o_mem = _mem_attention(act3, mkv16.reshape(b, n_mem, 2 * W_BR), b, t, _A_MQ, tq=tk)
        xp = _merge(xp, o_fox.reshape(b * t, W_BR), o_diff.reshape(b * t, W_BR),
                    o_mem.reshape(b * t, W_BR), act, merge_cols, *merge_w, l, alpha)
        outs[2].append(logf.reshape(b, t, H_FOX))
        outs[5].append(mkv32[0].reshape(b, n_mem, H_MEM, D_MEM))
        outs[6].append(mkv32[1].reshape(b, n_mem, H_MEM, D_MEM))

        act = _matmul(xs, w_act, l, 3072, BF16)
        kv32 = _matmul(xs, w_kv, l, W_BR, F32).reshape(m_s, 4, W_BR)
        logf = _logf_proj(xs, w_ff, b_ff, l)[:, :H_FOX]
        new_fk = kv32[:, _KV_FK].reshape(n_seq, t_new, H_FOX, D_FOX)
        new_fv = kv32[:, _KV_FV].reshape(n_seq, t_new, H_FOX, D_FOX)
        new_dk = kv32[:, _KV_DK].reshape(n_seq, t_new, H_DIFF, dv)
        new_dv = kv32[:, _KV_DV].reshape(n_seq, t_new, H_DIFF, dv)
        new_fl = logf.reshape(n_seq, t_new, H_FOX)
        act4 = act.reshape(n_seq, t_new, -1)
        qf = jnp.swapaxes(act4[..., _A_FQ * W_BR:(_A_FQ + 1) * W_BR].reshape(n_seq, t_new, H_FOX, D_FOX), 1, 2)
        dq = act4[..., _A_DQ * W_BR:(_A_DQ + 1) * W_BR].reshape(n_seq, t_new, H_DIFF, dv)
        dq2 = jnp.stack([jnp.where(lane_lo, dq, 0), jnp.where(lane_lo, 0, dq)], axis=0)
        qd = jnp.transpose(dq2, (1, 3, 0, 2, 4)).reshape(n_seq, H_DIFF * 2 * t_new, dv)
        qm = jnp.swapaxes(act4[..., _A_MQ * W_BR:(_A_MQ + 1) * W_BR].reshape(n_seq, t_new, H_MEM, D_MEM), 1, 2)
        qm = qm.reshape(n_seq, H_MEM * t_new, D_MEM)
        nfl = jnp.pad(jnp.swapaxes(new_fl, 1, 2), ((0, 0), (0, 0), (0, page - t_new)))
        o_fox, o_diff, o_mem = _sample_attention(
            l, page_table, qf, qd, qm, caches,
            (jnp.swapaxes(new_fk, 1, 2), jnp.swapaxes(new_fv, 1, 2), nfl, new_dk, new_dv),
            cache_mem_k, cache_mem_v, slopes, lamv, g, lam_init)
        o_fox = jnp.swapaxes(o_fox, 1, 2).reshape(m_s, W_BR)
        xs = _merge(xs, o_fox, o_diff.reshape(m_s, W_BR), o_mem.reshape(m_s, W_BR),
                    act, merge_cols, *merge_w, l, alpha)
        outs[7].append(new_fk)
        outs[8].append(new_fv)
        outs[9].append(new_fl)
        outs[10].append(new_dk)
        outs[11].append(new_dv)

    stacked = [jnp.stack(o) if o else None for o in outs]
    stacked[0] = kv_stacks[_KV_FK].reshape(depth, b, t, H_FOX, D_FOX)
    stacked[1] = kv_stacks[_KV_FV].reshape(depth, b, t, H_FOX, D_FOX)
    stacked[3] = kv_stacks[_KV_DK].reshape(depth, b, t, H_DIFF, dv)
    stacked[4] = kv_stacks[_KV_DV].reshape(depth, b, t, H_DIFF, dv)
    return (xp.reshape(b, t, D_MODEL), xs.reshape(n_seq, t_new, D_MODEL)) + tuple(stacked)
```

```python
import functools
import math

import jax
import jax.numpy as jnp
from jax import lax
from jax.experimental import pallas as pl
from jax.experimental.pallas import tpu as pltpu

F32 = jnp.float32
BF16 = jnp.bfloat16

D_MODEL = 1024
H_FOX, D_FOX = 8, 64
H_DIFF, D_DIFF = 4, 64
H_MEM, D_MEM = 4, 128
W_BR = 512
N_BRANCH = 3
LN_EPS = 1e-5
RMS_EPS = 1e-5
LANES = 128
NEG = -1e30

VMEM_LIMIT = 48 * 1024 * 1024


def _log2(n):
    assert n & (n - 1) == 0, n
    return n.bit_length() - 1


def _cparams(sem):
    return pltpu.CompilerParams(dimension_semantics=sem, vmem_limit_bytes=VMEM_LIMIT)


def _mm_body(x_ref, w_ref, *o_refs):
    r = jnp.dot(x_ref[...].astype(BF16), w_ref[0], preferred_element_type=F32)
    for o in o_refs:
        o[...] = r.astype(o.dtype).reshape(o.shape)


def _matmul(x, w, l, tn, out_dtype, tm=512):
    m, k = x.shape
    n = w.shape[2]
    tm = min(tm, m)
    return pl.pallas_call(
        _mm_body,
        grid=(n // tn, m // tm),
        in_specs=[pl.BlockSpec((tm, k), lambda j, i: (i, 0)),
                  pl.BlockSpec((1, k, tn), lambda j, i: (l, 0, j))],
        out_specs=pl.BlockSpec((tm, tn), lambda j, i: (i, j)),
        out_shape=jax.ShapeDtypeStruct((m, n), out_dtype),
        name="proj",
        compiler_params=_cparams(("parallel", "parallel")),
    )(x, w)


def _matmul_dual(x, w, l, tn, tm=512):
    m, k = x.shape
    n = w.shape[2]
    tm = min(tm, m)
    return pl.pallas_call(
        _mm_body,
        grid=(n // tn, m // tm),
        in_specs=[pl.BlockSpec((tm, k), lambda j, i: (i, 0)),
                  pl.BlockSpec((1, k, tn), lambda j, i: (l, 0, j))],
        out_specs=[pl.BlockSpec((1, tm, tn), lambda j, i: (j, i, 0)),
                   pl.BlockSpec((tm, tn), lambda j, i: (i, j))],
        out_shape=[jax.ShapeDtypeStruct((n // tn, m, tn), F32),
                   jax.ShapeDtypeStruct((m, n), BF16)],
        name="proj_mem_kv",
        compiler_params=_cparams(("parallel", "parallel")),
    )(x, w)


def _kv_body(x_ref, w_ref, *refs, n_out, n_tr):
    o_refs = refs[-(n_out + 1):]
    r = jnp.dot(x_ref[...].astype(BF16), w_ref[0], preferred_element_type=F32)
    for n in range(n_out):
        rn = r[:, n * W_BR:(n + 1) * W_BR]
        if n < n_tr:
            o_refs[n][0, 0] = rn.T
        else:
            o_refs[n][0] = rn.reshape(o_refs[n].shape[1:])
    o_refs[n_out][...] = r.astype(BF16)


def _proj_kv(x, w, l, stacks, b, t, n_tr, tm=512):
    m, k = x.shape
    depth, _, n = w.shape
    n_out = n // W_BR
    tm = min(tm, t)
    per_b = t // tm
    in_specs = [pl.BlockSpec((tm, k), lambda i: (i, 0)),
                pl.BlockSpec((1, k, n), lambda i: (l, 0, 0))]
    operands = [x, w]
    aliases = {}
    if stacks is not None:
        in_specs += [pl.BlockSpec(memory_space=pl.ANY)] * n_out
        operands += list(stacks)
        aliases = {2 + i: i for i in range(n_out)}
    tr_spec = pl.BlockSpec((1, 1, W_BR, tm), lambda i: (l, i // per_b, 0, i % per_b))
    row_spec = pl.BlockSpec((1, tm, H_DIFF, 2 * D_DIFF), lambda i: (l, i, 0, 0))
    res = pl.pallas_call(
        functools.partial(_kv_body, n_out=n_out, n_tr=n_tr),
        grid=(m // tm,),
        in_specs=in_specs,
        out_specs=[tr_spec] * n_tr + [row_spec] * (n_out - n_tr) + [pl.BlockSpec((tm, n), lambda i: (i, 0))],
        out_shape=[jax.ShapeDtypeStruct((depth, b, W_BR, t), F32)] * n_tr
                  + [jax.ShapeDtypeStruct((depth, m, H_DIFF, 2 * D_DIFF), F32)] * (n_out - n_tr)
                  + [jax.ShapeDtypeStruct((m, n), BF16)],
        input_output_aliases=aliases,
        name="proj_kv",
        compiler_params=_cparams(("parallel",)),
    )(*operands)
    return tuple(res[:n_out]), res[n_out]


def _logf_body(x_ref, w_ref, b_ref, o_ref):
    z = jnp.dot(x_ref[...].astype(BF16), w_ref[0], preferred_element_type=F32) + b_ref[0]
    o_ref[...] = jnp.minimum(z, 0.0) - jnp.log1p(jnp.exp(-jnp.abs(z)))


def _logf_proj(x, w_pad, b_pad, l, tm=512):
    m, k = x.shape
    tm = min(tm, m)
    return pl.pallas_call(
        _logf_body,
        grid=(m // tm,),
        in_specs=[pl.BlockSpec((tm, k), lambda i: (i, 0)),
                  pl.BlockSpec((1, k, LANES), lambda i: (l, 0, 0)),
                  pl.BlockSpec((1, 1, LANES), lambda i: (l, 0, 0))],
        out_specs=pl.BlockSpec((tm, LANES), lambda i: (i, 0)),
        out_shape=jax.ShapeDtypeStruct((m, LANES), F32),
        name="proj_logf",
        compiler_params=_cparams(("parallel",)),
    )(x, w_pad, b_pad)


def _dot01(a, b01):
    hi = a.astype(BF16)
    r1 = a - hi.astype(F32)
    mid = r1.astype(BF16)
    lo = (r1 - mid.astype(F32)).astype(BF16)
    d = functools.partial(jnp.dot, preferred_element_type=F32)
    return d(hi, b01) + d(mid, b01) + d(lo, b01)


def _upper_ones(n):
    r = lax.broadcasted_iota(jnp.int32, (n, n), 0)
    c = lax.broadcasted_iota(jnp.int32, (n, n), 1)
    return jnp.where(r <= c, 1.0, 0.0).astype(BF16)


def _cumsum_body(x_ref, o_ref):
    x = x_ref[0]
    rows = x.shape[0]
    within = _dot01(x, _upper_ones(LANES))
    r = lax.broadcasted_iota(jnp.int32, (rows, rows), 0)
    c = lax.broadcasted_iota(jnp.int32, (rows, rows), 1)
    strict_lower = jnp.where(c < r, 1.0, 0.0).astype(BF16)
    x_hi = x.astype(BF16)
    r1 = x - x_hi.astype(F32)
    x_mid = r1.astype(BF16)
    x_lo = (r1 - x_mid.astype(F32)).astype(BF16)
    d = functools.partial(jnp.dot, preferred_element_type=F32)
    prev = d(strict_lower, x_hi) + d(strict_lower, x_mid) + d(strict_lower, x_lo)
    all_ones = jnp.ones((LANES, LANES), BF16)
    o_ref[0] = within + _dot01(prev, all_ones)


def _cumsum_seq(x):
    n, t = x.shape
    rows = t // LANES
    x3 = x.reshape(n, rows, LANES)
    out = pl.pallas_call(
        _cumsum_body,
        grid=(n,),
        in_specs=[pl.BlockSpec((1, rows, LANES), lambda i: (i, 0, 0))],
        out_specs=pl.BlockSpec((1, rows, LANES), lambda i: (i, 0, 0)),
        out_shape=jax.ShapeDtypeStruct((n, rows, LANES), F32),
        name="logf_cumsum",
        compiler_params=_cparams(("parallel",)),
    )(x3)
    return out.reshape(n, t)


def _nt_dot(a, b):
    return lax.dot_general(a, b, (((1,), (1,)), ((), ())), preferred_element_type=F32)


def _tile_lanes(x, n):
    return x if n == 1 else jnp.concatenate([x] * n, axis=1)


def _softmax_step(s, m_ref, acc_ref, v_chunk):
    m_old = m_ref[...]
    m_new = jnp.maximum(m_old, jnp.broadcast_to(jnp.max(s, axis=1, keepdims=True), m_old.shape))
    alpha = jnp.exp(m_old - m_new)
    p = jnp.exp(s - _tile_lanes(m_new, s.shape[1] // LANES)).astype(BF16)
    acc_ref[...] = (_tile_lanes(alpha, acc_ref.shape[-1] // LANES) * acc_ref[...]
                    + jnp.dot(p, v_chunk, preferred_element_type=F32))
    m_ref[...] = m_new


def _causal_sweep(i, tq, tk, n_streams, scores, consume, s_ref):
    per_q = tq // tk
    assert per_q % 2 == 0, "chunk c lives in half c % 2 of s_ref, kept static by pairing chunks"

    def stage(c, slot):
        for st in range(n_streams):
            s_ref[slot, st] = scores(c, st)

    def chunk_pair(jj, carry):
        c = 2 * jj
        stage(c + 1, 1)
        for st in range(n_streams):
            consume(c, st, s_ref[0, st], None)
        stage(c + 2, 0)
        for st in range(n_streams):
            consume(c + 1, st, s_ref[1, st], None)
        return carry

    stage(0, 0)
    lax.fori_loop(0, i * (per_q // 2), chunk_pair, 0)
    for r in range(per_q):
        c = i * per_q + r
        if r + 1 < per_q:
            stage(c + 1, (r + 1) % 2)
        for st in range(n_streams):
            consume(c, st, s_ref[r % 2, st, pl.ds(r * tk, tq - r * tk), :], r * tk)


def _causal_mask(s):
    row = lax.broadcasted_iota(jnp.int32, s.shape, 0)
    col = lax.broadcasted_iota(jnp.int32, s.shape, 1)
    return jnp.where(col <= row, s, NEG)


def _rows_from(ref, lead, row0):
    if row0 is None:
        return ref.at[lead]
    return ref.at[lead, pl.ds(row0, ref.shape[1] - row0)]


def _fox_body(q_ref, k_ref, v_ref, f_ref, o_ref, vaug_ref, acc_ref, m_ref, s_ref, *, tq, tk):
    i = pl.program_id(2)
    t_all = k_ref.shape[1]
    half = D_FOX

    @pl.when(i == 0)
    def _():
        for c in range(t_all // tk):
            sl = slice(c * tk, (c + 1) * tk)
            v = v_ref[0, sl, :].astype(F32)
            lane = lax.broadcasted_iota(jnp.int32, v.shape, 1)
            v0 = jnp.where(lane < half, v, jnp.where(lane == half, 1.0, 0.0))
            v1 = jnp.where(lane >= half, v, jnp.where(lane == 0, 1.0, 0.0))
            vaug_ref[0, sl, :] = v0.astype(BF16)
            vaug_ref[1, sl, :] = v1.astype(BF16)

    q = q_ref[0]
    qlane = lax.broadcasted_iota(jnp.int32, q.shape, 1)
    q_heads = [jnp.where((qlane >= half) == (hh == 1), q, jnp.zeros_like(q)) for hh in range(2)]
    m_ref[...] = jnp.full(m_ref.shape, NEG, F32)
    acc_ref[...] = jnp.zeros(acc_ref.shape, F32)

    def scores(c, hh):
        start = pl.multiple_of(c * tk, tk)
        return _nt_dot(q_heads[hh], k_ref[0, pl.ds(start, tk), :]) - f_ref[0, hh, pl.ds(c, 1), :]

    def consume(c, hh, s, row0):
        start = pl.multiple_of(c * tk, tk)
        if row0 is not None:
            s = _causal_mask(s)
        _softmax_step(s, _rows_from(m_ref, hh, row0), _rows_from(acc_ref, hh, row0),
                      vaug_ref[hh, pl.ds(start, tk), :])

    _causal_sweep(i, tq, tk, 2, scores, consume, s_ref)
    acc0 = acc_ref[0]
    acc1 = acc_ref[1]
    o0 = acc0 * (1.0 / acc0[:, half:half + 1])
    o1 = acc1 * (1.0 / acc1[:, 0:1])
    lane = lax.broadcasted_iota(jnp.int32, o0.shape, 1)
    o_ref[0] = jnp.where(lane < half, o0, o1).astype(o_ref.dtype)


def _fox_attention(qa, kva, fcum, b, t, q_col0, k_col0, v_col0, tq, tk):
    npair = H_FOX // 2
    return pl.pallas_call(
        functools.partial(_fox_body, tq=tq, tk=tk),
        grid=(b, npair, t // tq),
        in_specs=[pl.BlockSpec((1, tq, LANES), lambda bi, g, i: (bi, i, q_col0 + g)),
                  pl.BlockSpec((1, t, LANES), lambda bi, g, i: (bi, 0, k_col0 + g)),
                  pl.BlockSpec((1, t, LANES), lambda bi, g, i: (bi, 0, v_col0 + g)),
                  pl.BlockSpec((1, 2, t // tk, tk), lambda bi, g, i: (bi, g, 0, 0))],
        out_specs=pl.BlockSpec((1, tq, LANES), lambda bi, g, i: (bi, i, g)),
        out_shape=jax.ShapeDtypeStruct((b, t, W_BR), BF16),
        scratch_shapes=[pltpu.VMEM((2, t, LANES), BF16),
                        pltpu.VMEM((2, tq, LANES), F32),
                        pltpu.VMEM((2, tq, LANES), F32),
                        pltpu.VMEM((2, 2, tq, tk), F32)],
        name="fox_attn",
        compiler_params=_cparams(("parallel", "parallel", "arbitrary")),
    )(qa, kva, kva, fcum)


def _lambda_value(lam_ref, lam_init):
    lv = lam_ref[0]
    a = jnp.sum(lv[0:1] * lv[1:2], axis=1, keepdims=True)
    c = jnp.sum(lv[2:3] * lv[3:4], axis=1, keepdims=True)
    return jnp.exp(a) - jnp.exp(c) + lam_init


def _rms_scale(o, g, lam_init):
    ms = jnp.mean(o * o, axis=-1, keepdims=True)
    return o * lax.rsqrt(ms + RMS_EPS) * g * (1.0 - lam_init)


def _diff_body(slope_ref, q_ref, k_ref, v_ref, lam_ref, g_ref, o_ref,
               vaug_ref, acc_ref, m_ref, s_ref, *, tq, tk, lam_init):
    h = pl.program_id(1)
    i = pl.program_id(2)
    t_all = k_ref.shape[1]
    half = D_DIFF
    dv = 2 * D_DIFF

    @pl.when(i == 0)
    def _():
        for c in range(t_all // tk):
            sl = slice(c * tk, (c + 1) * tk)
            vaug_ref[sl, :dv] = v_ref[0, sl, :]
            lane = lax.broadcasted_iota(jnp.int32, (tk, LANES), 1)
            vaug_ref[sl, dv:] = jnp.where(lane == 0, 1.0, 0.0).astype(BF16)

    slope = slope_ref[h]
    q = q_ref[0]
    qlane = lax.broadcasted_iota(jnp.int32, q.shape, 1)
    q_maps = [jnp.where((qlane >= half) == (mm == 1), q, jnp.zeros_like(q)) for mm in range(2)]
    m_ref[...] = jnp.full(m_ref.shape, NEG, F32)
    acc_ref[...] = jnp.zeros(acc_ref.shape, F32)

    def scores(c, mm):
        start = pl.multiple_of(c * tk, tk)
        rel = lax.broadcasted_iota(jnp.int32, (1, tk), 1) + (c * tk - i * tq)
        return _nt_dot(q_maps[mm], k_ref[0, pl.ds(start, tk), :]) + slope * rel.astype(F32)

    def consume(c, mm, s, row0):
        start = pl.multiple_of(c * tk, tk)
        if row0 is not None:
            s = _causal_mask(s)
        _softmax_step(s, _rows_from(m_ref, mm, row0), _rows_from(acc_ref, mm, row0),
                      vaug_ref[pl.ds(start, tk), :])

    _causal_sweep(i, tq, tk, 2, scores, consume, s_ref)
    a0 = acc_ref[0]
    a1 = acc_ref[1]
    o0 = a0[:, :dv] * (1.0 / a0[:, dv:dv + 1])
    o1 = a1[:, :dv] * (1.0 / a1[:, dv:dv + 1])
    lam = _lambda_value(lam_ref, lam_init)
    o_ref[0] = _rms_scale(o0 - lam * o1, g_ref[0], lam_init).astype(o_ref.dtype)


def _diff_attention(qa, kva, slopes, lamv, g, l, b, t, q_col0, k_col0, v_col0, lam_init, tq, tk):
    dv = 2 * D_DIFF
    return pl.pallas_call(
        functools.partial(_diff_body, tq=tq, tk=tk, lam_init=lam_init),
        grid=(b, H_DIFF, t // tq),
        in_specs=[pl.BlockSpec(memory_space=pltpu.SMEM),
                  pl.BlockSpec((1, tq, LANES), lambda bi, h, i: (bi, i, q_col0 + h)),
                  pl.BlockSpec((1, t, LANES), lambda bi, h, i: (bi, 0, k_col0 + h)),
                  pl.BlockSpec((1, t, LANES), lambda bi, h, i: (bi, 0, v_col0 + h)),
                  pl.BlockSpec((1, 4, D_DIFF), lambda bi, h, i: (l, 0, 0)),
                  pl.BlockSpec((1, 1, dv), lambda bi, h, i: (l, 0, 0))],
        out_specs=pl.BlockSpec((1, tq, LANES), lambda bi, h, i: (bi, i, h)),
        out_shape=jax.ShapeDtypeStruct((b, t, W_BR), BF16),
        scratch_shapes=[pltpu.VMEM((t, dv + LANES), BF16),
                        pltpu.VMEM((2, tq, dv + LANES), F32),
                        pltpu.VMEM((2, tq, LANES), F32),
                        pltpu.VMEM((2, 2, tq, tk), F32)],
        name="diff_attn",
        compiler_params=_cparams(("parallel", "parallel", "arbitrary")),
    )(slopes, qa, kva, kva, lamv, g)


def _mem_body(q_ref, k_ref, v_ref, o_ref):
    scale = D_MEM ** -0.5
    for h in range(H_MEM):
        sl = slice(h * D_MEM, (h + 1) * D_MEM)
        s = _nt_dot(q_ref[0, :, sl], k_ref[0, :, sl]) * scale
        m = jnp.max(s, axis=1, keepdims=True)
        p = jnp.exp(s - m)
        l = jnp.sum(p, axis=1, keepdims=True)
        o = jnp.dot(p.astype(BF16), v_ref[0, :, sl], preferred_element_type=F32)
        o_ref[0, :, sl] = (o * (1.0 / l)).astype(o_ref.dtype)


def _mem_attention(qa, mkv, b, t, q_col0, tq=512):
    n_mem = mkv.shape[1]
    return pl.pallas_call(
        _mem_body,
        grid=(b, t // tq),
        in_specs=[pl.BlockSpec((1, tq, W_BR), lambda bi, i: (bi, i, q_col0)),
                  pl.BlockSpec((1, n_mem, W_BR), lambda bi, i: (bi, 0, 0)),
                  pl.BlockSpec((1, n_mem, W_BR), lambda bi, i: (bi, 0, 1))],
        out_specs=pl.BlockSpec((1, tq, W_BR), lambda bi, i: (bi, i, 0)),
        out_shape=jax.ShapeDtypeStruct((b, t, W_BR), BF16),
        name="mem_attn",
        compiler_params=_cparams(("parallel", "parallel")),
    )(qa, mkv, mkv)


def _silu_gate(o_ref, g_ref):
    g = g_ref[...].astype(F32)
    return (o_ref[...].astype(F32) * (g * jax.nn.sigmoid(g))).astype(BF16)


def _merge_body(x_ref, of_ref, od_ref, om_ref, fg_ref, dg_ref, mg_ref, gates_ref,
                wpf_ref, wpd_ref, wpm_ref, wo_ref, lng_ref, lnb_ref, y_ref, *, alpha):
    d = functools.partial(jnp.dot, preferred_element_type=F32)
    merged = None
    for n, (o_ref, g_ref, w_ref) in enumerate(((of_ref, fg_ref, wpf_ref),
                                               (od_ref, dg_ref, wpd_ref),
                                               (om_ref, mg_ref, wpm_ref))):
        gate = jax.nn.sigmoid(gates_ref[:, n * D_MODEL:(n + 1) * D_MODEL].astype(F32))
        term = gate * d(_silu_gate(o_ref, g_ref), w_ref[0])
        merged = term if merged is None else merged + term
    z = alpha * x_ref[...] + d(merged.astype(BF16), wo_ref[0])
    mu = jnp.mean(z, axis=-1, keepdims=True)
    zc = z - mu
    var = jnp.mean(zc * zc, axis=-1, keepdims=True)
    y_ref[...] = zc * lax.rsqrt(var + LN_EPS) * lng_ref[0] + lnb_ref[0]


def _merge(x, o_fox, o_diff, o_mem, proj, cols, wpf, wpd, wpm, wo, ln_g, ln_b, l, alpha, tm=512):
    m = x.shape[0]
    tm = min(tm, m)
    fg_c, dg_c, mg_c, gates_c = cols
    row = lambda i: (i, 0)
    layer = lambda i: (l, 0, 0)
    return pl.pallas_call(
        functools.partial(_merge_body, alpha=alpha),
        grid=(m // tm,),
        in_specs=[pl.BlockSpec((tm, D_MODEL), row),
                  pl.BlockSpec((tm, W_BR), row),
                  pl.BlockSpec((tm, W_BR), row),
                  pl.BlockSpec((tm, W_BR), row),
                  pl.BlockSpec((tm, W_BR), lambda i: (i, fg_c)),
                  pl.BlockSpec((tm, W_BR), lambda i: (i, dg_c)),
                  pl.BlockSpec((tm, W_BR), lambda i: (i, mg_c)),
                  pl.BlockSpec((tm, N_BRANCH * D_MODEL), lambda i: (i, gates_c)),
                  pl.BlockSpec((1, W_BR, D_MODEL), layer),
                  pl.BlockSpec((1, W_BR, D_MODEL), layer),
                  pl.BlockSpec((1, W_BR, D_MODEL), layer),
                  pl.BlockSpec((1, D_MODEL, D_MODEL), layer),
                  pl.BlockSpec((1, 1, D_MODEL), layer),
                  pl.BlockSpec((1, 1, D_MODEL), layer)],
        out_specs=pl.BlockSpec((tm, D_MODEL), row),
        out_shape=jax.ShapeDtypeStruct((m, D_MODEL), F32),
        name="merge",
        compiler_params=_cparams(("parallel",)),
    )(x, o_fox, o_diff, o_mem, proj, proj, proj, proj, wpf, wpd, wpm, wo, ln_g, ln_b)


PAGES_PER_STEP = 8


def _masked_softmax_step(s, valid, m_ref, l_ref):
    if valid is not None:
        s = jnp.where(valid, s, NEG)
    m_old = m_ref[...]
    m_new = jnp.maximum(m_old, jnp.broadcast_to(jnp.max(s, axis=1, keepdims=True), m_old.shape))
    alpha = jnp.exp(m_old - m_new)
    p = jnp.exp(s - _tile_lanes(m_new, s.shape[1] // LANES))
    if valid is not None:
        p = jnp.where(valid, p, 0.0)
    l_ref[...] = alpha * l_ref[...] + jnp.broadcast_to(jnp.sum(p, axis=1, keepdims=True), m_old.shape)
    m_ref[...] = m_new
    return alpha, p


def _sample_body(pt_ref, qf_ref, qd_ref, qm_ref, *refs, n_steps, pp, page, t_new, lam_init):
    del pt_ref
    cfk, cfv, cfl, cdk, cdv = (refs[n * pp:(n + 1) * pp] for n in range(5))
    (nfk_ref, nfv_ref, nfl_ref, ndk_ref, ndv_ref, mk_ref, mv_ref, slope_ref, lam_ref, g_ref,
     of_ref, od_ref, om_ref, mf_ref, lf_ref, af_ref, md_ref, ld_ref, ad_ref, fc_ref) = refs[5 * pp:]
    step_id = pl.program_id(1)
    dv = 2 * D_DIFF
    rows = H_FOX * t_new
    nd = page * H_DIFF
    n_mem = mk_ref.shape[2]

    def head_match(n_rows, n_cols, rows_per_head, n_head):
        r = lax.broadcasted_iota(jnp.int32, (n_rows, n_cols), 0)
        c = lax.broadcasted_iota(jnp.int32, (n_rows, n_cols), 1)
        return (c & (n_head - 1)) == (r >> _log2(rows_per_head)), r, c

    @pl.when(step_id == 0)
    def _():
        mf_ref[...] = jnp.full(mf_ref.shape, NEG, F32)
        md_ref[...] = jnp.full(md_ref.shape, NEG, F32)
        for r in (lf_ref, af_ref, ld_ref, ad_ref, fc_ref):
            r[...] = jnp.zeros(r.shape, F32)
        mk = mk_ref[0, 0].reshape(n_mem * H_MEM, D_MEM).astype(BF16)
        mv = mv_ref[0, 0].reshape(n_mem * H_MEM, D_MEM).astype(BF16)
        valid, _, _ = head_match(H_MEM * t_new, n_mem * H_MEM, t_new, H_MEM)
        s = jnp.where(valid, _nt_dot(qm_ref[0], mk) * (D_MEM ** -0.5), NEG)
        p = jnp.where(valid, jnp.exp(s - jnp.max(s, axis=1, keepdims=True)), 0.0)
        l = jnp.sum(p, axis=1, keepdims=True)
        o = jnp.dot(p.astype(BF16), mv, preferred_element_type=F32) * (1.0 / l)
        for h in range(H_MEM):
            om_ref[0, :, h * D_MEM:(h + 1) * D_MEM] = o[h * t_new:(h + 1) * t_new].astype(om_ref.dtype)

    def attend(pages, new):
        n_pg = len(pages)
        carry = fc_ref[...]
        fcums = []
        for pg in pages:
            fcums.append(carry + _dot01(pg["fl"], _upper_ones(page)))
            carry = jnp.broadcast_to(fcums[-1][:, page - 1:page], carry.shape)
        fc_ref[...] = carry
        fbias = [jnp.concatenate([jnp.broadcast_to(fc[h:h + 1, :], (t_new, page)) for h in range(H_FOX)], axis=0)
                 for fc in fcums]
        s = jnp.concatenate([pg["fox_scores"]() - fb for pg, fb in zip(pages, fbias)], axis=1)
        valid = None
        if new:
            r = lax.broadcasted_iota(jnp.int32, s.shape, 0)
            c = lax.broadcasted_iota(jnp.int32, s.shape, 1)
            valid = c <= (r & (t_new - 1))
        alpha, p = _masked_softmax_step(s, valid, mf_ref, lf_ref)
        parts = [pg["fox_pv"](p[:, n * page:(n + 1) * page].astype(BF16)) for n, pg in enumerate(pages)]
        af_ref[...] = _tile_lanes(alpha, W_BR // LANES) * af_ref[...] + functools.reduce(jnp.add, parts)
        valid, r, c = head_match(rows, n_pg * nd, 2 * t_new, H_DIFF)
        if new:
            valid = valid & ((c >> _log2(H_DIFF)) <= (r & (t_new - 1)))
        c1 = lax.broadcasted_iota(jnp.int32, (1, nd), 1)
        slope_row = jnp.zeros((1, nd), F32)
        for h in range(H_DIFF):
            slope_row = jnp.where((c1 & (H_DIFF - 1)) == h, slope_ref[h], slope_row)
        rel = c1 >> _log2(H_DIFF)
        s = jnp.concatenate([_nt_dot(qd_ref[0], pg["dk"]) + slope_row * (rel + pg["pos0"]).astype(F32)
                             for pg in pages], axis=1)
        alpha, p = _masked_softmax_step(s, valid, md_ref, ld_ref)
        parts = [jnp.dot(p[:, n * nd:(n + 1) * nd].astype(BF16), pg["dv"], preferred_element_type=F32)
                 for n, pg in enumerate(pages)]
        ad_ref[...] = alpha * ad_ref[...] + functools.reduce(jnp.add, parts)

    attend([dict(
        fox_scores=lambda r=r: jnp.dot(qf_ref[0], cfk[r][0, 0].reshape(W_BR, page).astype(BF16),
                                       preferred_element_type=F32),
        fox_pv=lambda p, r=r: _nt_dot(p, cfv[r][0, 0].reshape(W_BR, page).astype(BF16)),
        fl=cfl[r][0, 0],
        dk=cdk[r][0, 0].reshape(nd, dv).astype(BF16),
        dv=cdv[r][0, 0].reshape(nd, dv).astype(BF16),
        pos0=(step_id * pp + r - n_steps * pp) * page) for r in range(pp)], False)

    @pl.when(step_id == n_steps - 1)
    def _():
        def pad_rows(x, n):
            return jnp.concatenate([x, jnp.zeros((n - x.shape[0], x.shape[1]), x.dtype)], axis=0).astype(BF16)

        attend([dict(
            fox_scores=lambda: _nt_dot(qf_ref[0], pad_rows(nfk_ref[0], page)),
            fox_pv=lambda p: jnp.dot(p, pad_rows(nfv_ref[0], page), preferred_element_type=F32),
            fl=nfl_ref[0],
            dk=pad_rows(ndk_ref[0].reshape(t_new * H_DIFF, dv), nd),
            dv=pad_rows(ndv_ref[0].reshape(t_new * H_DIFF, dv), nd),
            pos0=0)], True)
        o3 = (af_ref[...] * (1.0 / _tile_lanes(lf_ref[...], W_BR // LANES))).reshape(H_FOX, t_new, W_BR)
        own = (lax.broadcasted_iota(jnp.int32, o3.shape, 0)
               == lax.broadcasted_iota(jnp.int32, o3.shape, 2) >> _log2(D_FOX))
        of_ref[0] = jnp.sum(jnp.where(own, o3, 0.0), axis=0).astype(of_ref.dtype)
        lam = _lambda_value(lam_ref, lam_init)
        o = ad_ref[...] * (1.0 / ld_ref[...])
        for h in range(H_DIFF):
            o1 = o[h * 2 * t_new:h * 2 * t_new + t_new]
            o2 = o[h * 2 * t_new + t_new:(h + 1) * 2 * t_new]
            od_ref[0, :, h * dv:(h + 1) * dv] = _rms_scale(o1 - lam * o2, g_ref[0], lam_init).astype(od_ref.dtype)


def _sample_attention(l, page_table, qf, qd, qm, caches, news, mem_k, mem_v, slopes, lamv, g, lam_init):
    cfk, cfv, cfl, cdk, cdv = caches
    n_seq, n_pages = page_table.shape
    page = cdk.shape[2]
    t_new = news[0].shape[1]
    n_mem = mem_k.shape[2]
    pp = PAGES_PER_STEP
    assert n_pages % pp == 0, (n_pages, pp)
    n_steps = n_pages // pp
    dv = 2 * D_DIFF

    def cache_spec(shape, r):
        return pl.BlockSpec((1, 1) + shape,
                            lambda s, p, pt, r=r: (l, pt[s * n_pages + p * pp + r]) + (0,) * len(shape))

    def seq_spec(shape):
        return pl.BlockSpec((1,) + shape, lambda s, p, pt: (s,) + (0,) * len(shape))

    layer3 = lambda s, p, pt: (l, 0, 0)
    mem_spec = pl.BlockSpec((1, 1, n_mem, H_MEM, D_MEM), lambda s, p, pt: (l, s, 0, 0, 0))
    rows = H_FOX * t_new
    in_specs = [seq_spec((rows, W_BR)), seq_spec((rows, dv)), seq_spec((H_MEM * t_new, D_MEM))]
    operands = [qf, qd, qm]
    for arr, shape in ((cfk, (H_FOX, D_FOX, page)), (cfv, (H_FOX, D_FOX, page)), (cfl, (H_FOX, page)),
                       (cdk, (page, H_DIFF, dv)), (cdv, (page, H_DIFF, dv))):
        for r in range(pp):
            in_specs.append(cache_spec(shape, r))
            operands.append(arr)
    in_specs += [seq_spec((t_new, W_BR)), seq_spec((t_new, W_BR)), seq_spec((H_FOX, page)),
                 seq_spec((t_new, H_DIFF, dv)), seq_spec((t_new, H_DIFF, dv)),
                 mem_spec, mem_spec,
                 pl.BlockSpec(memory_space=pltpu.SMEM),
                 pl.BlockSpec((1, 4, D_DIFF), layer3), pl.BlockSpec((1, 1, dv), layer3)]
    operands += list(news) + [mem_k, mem_v, slopes, lamv, g]
    grid_spec = pltpu.PrefetchScalarGridSpec(
        num_scalar_prefetch=1,
        grid=(n_seq, n_steps),
        in_specs=in_specs,
        out_specs=[seq_spec((t_new, W_BR)), seq_spec((t_new, W_BR)), seq_spec((t_new, W_BR))],
        scratch_shapes=[pltpu.VMEM((rows, LANES), F32), pltpu.VMEM((rows, LANES), F32),
                        pltpu.VMEM((rows, W_BR), F32),
                        pltpu.VMEM((rows, LANES), F32), pltpu.VMEM((rows, LANES), F32),
                        pltpu.VMEM((rows, dv), F32),
                        pltpu.VMEM((8, LANES), F32)])
    return pl.pallas_call(
        functools.partial(_sample_body, n_steps=n_steps, pp=pp, page=page, t_new=t_new, lam_init=lam_init),
        grid_spec=grid_spec,
        out_shape=[jax.ShapeDtypeStruct((n_seq, t_new, W_BR), BF16),
                   jax.ShapeDtypeStruct((n_seq, t_new, W_BR), BF16),
                   jax.ShapeDtypeStruct((n_seq, t_new, W_BR), BF16)],
        name="sample_attn",
        compiler_params=_cparams(("parallel", "arbitrary")),
    )(page_table.reshape(-1), *operands)


_A_FQ, _A_FG, _A_DQ, _A_DG, _A_MQ, _A_MG = range(6)
_A_GATES = 6 * W_BR
_KV_FK, _KV_FV, _KV_DK, _KV_DV = range(4)


def _split_w_in(w):
    sizes = (W_BR, W_BR, W_BR, W_BR, H_FOX, W_BR, W_BR, W_BR, W_BR, W_BR, W_BR, N_BRANCH * D_MODEL)
    offs = [0]
    for s in sizes:
        offs.append(offs[-1] + s)
    fq, fk, fv, fg, ff, dq, dk, dv, dg, mq, mg, gates = [w[..., offs[n]:offs[n + 1]] for n in range(12)]
    w_act = jnp.concatenate([fq * (D_FOX ** -0.5), fg, dq * (D_DIFF ** -0.5), dg, mq, mg, gates],
                            axis=-1).astype(BF16)
    w_kv = jnp.concatenate([fk, fv, dk, dv], axis=-1).astype(BF16)
    w_ff = jnp.pad(ff, ((0, 0), (0, 0), (0, LANES - H_FOX))).astype(BF16)
    return w_act, w_kv, w_ff


def kernel(x_prompt, x_sample, mem_prompt, cache_fox_k, cache_fox_v, cache_fox_logf, cache_diff_k,
           cache_diff_v, cache_mem_k, cache_mem_v, page_table, w_in, b_forget, w_mem_kv, lambda_q1,
           lambda_k1, lambda_q2, lambda_k2, diff_norm_g, w_proj_fox, w_proj_diff, w_proj_mem, w_out,
           ln_g, ln_b):
    depth = w_in.shape[0]
    b, t, _ = x_prompt.shape
    n_seq, t_new, _ = x_sample.shape
    page = cache_fox_k.shape[2]
    n_mem = mem_prompt.shape[1]
    alpha = (2 * depth) ** 0.25
    tq = min(1024, t)
    tk = min(512, t)
    dv = 2 * D_DIFF
    nblk = W_BR // LANES

    slopes = 2.0 ** (-8.0 * jnp.arange(1, H_DIFF + 1, dtype=F32) / H_DIFF)
    cfk = jnp.transpose(cache_fox_k, (0, 1, 3, 4, 2))
    cfv = jnp.transpose(cache_fox_v, (0, 1, 3, 4, 2))
    cfl = jnp.swapaxes(cache_fox_logf, 2, 3)
    caches = (cfk, cfv, cfl, cache_diff_k, cache_diff_v)
    lane_lo = (jnp.arange(dv) < D_DIFF)
    head_eye = (jnp.arange(H_FOX)[:, None, None] == jnp.arange(W_BR)[None, None, :] // D_FOX)

    w_act, w_kv, w_ff = _split_w_in(w_in)
    w_mkv = w_mem_kv.astype(BF16)
    b_ff = jnp.pad(b_forget, ((0, 0), (0, LANES - H_FOX))).reshape(depth, 1, LANES)
    lamv = jnp.stack([lambda_q1, lambda_k1, lambda_q2, lambda_k2], axis=1)
    g = diff_norm_g.reshape(depth, 1, dv)
    merge_w = (w_proj_fox.astype(BF16), w_proj_diff.astype(BF16), w_proj_mem.astype(BF16),
               w_out.astype(BF16), ln_g.reshape(depth, 1, D_MODEL), ln_b.reshape(depth, 1, D_MODEL))
    merge_cols = (_A_FG, _A_DG, _A_MG, _A_GATES // (N_BRANCH * D_MODEL))

    xp = x_prompt.reshape(b * t, D_MODEL)
    xs = x_sample.reshape(n_seq * t_new, D_MODEL)
    m_s = n_seq * t_new
    mem2 = mem_prompt.reshape(b * n_mem, D_MODEL)
    outs = [[] for _ in range(12)]
    kv_stacks = None

    for l in range(depth):
        lam_init = 0.8 - 0.6 * math.exp(-0.3 * l)

        act = _matmul(xp, w_act, l, 3072, BF16)
        kv_stacks, kv16 = _proj_kv(xp, w_kv, l, kv_stacks, b, t, 2)
        logf = _logf_proj(xp, w_ff, b_ff, l)[:, :H_FOX]
        mkv32, mkv16 = _matmul_dual(mem2, w_mkv, l, W_BR)
        fcum = _cumsum_seq(jnp.swapaxes(logf.reshape(b, t, H_FOX), 1, 2).reshape(b * H_FOX, t))
        act3 = act.reshape(b, t, -1)
        kv3 = kv16.reshape(b, t, -1)
        o_fox = _fox_attention(act3, kv3, fcum.reshape(b, H_FOX, t // tk, tk), b, t,
                               _A_FQ * nblk, _KV_FK * nblk, _KV_FV * nblk, tq, tk)
        o_diff = _diff_attention(act3, kv3, slopes, lamv, g, l, b, t,
                                 _A_DQ * nblk, _KV_DK * nblk, _KV_DV * nblk, lam_init, tq, tk)
        o_mem = _mem_attention(act3, mkv16.reshape(b, n_mem, 2 * W_BR), b, t, _A_MQ, tq=tk)
        xp = _merge(xp, o_fox.reshape(b * t, W_BR), o_diff.reshape(b * t, W_BR),
                    o_mem.reshape(b * t, W_BR), act, merge_cols, *merge_w, l, alpha)
        outs[2].append(logf.reshape(b, t, H_FOX))
        outs[5].append(mkv32[0].reshape(b, n_mem, H_MEM, D_MEM))
        outs[6].append(mkv32[1].reshape(b, n_mem, H_MEM, D_MEM))

        act = _matmul(xs, w_act, l, 3072, BF16)
        kv32 = _matmul(xs, w_kv, l, W_BR, F32).reshape(m_s, 4, W_BR)
        logf = _logf_proj(xs, w_ff, b_ff, l)[:, :H_FOX]
        new_fk = kv32[:, _KV_FK].reshape(n_seq, t_new, H_FOX, D_FOX)
        new_fv = kv32[:, _KV_FV].reshape(n_seq, t_new, H_FOX, D_FOX)
        new_dk = kv32[:, _KV_DK].reshape(n_seq, t_new, H_DIFF, dv)
        new_dv = kv32[:, _KV_DV].reshape(n_seq, t_new, H_DIFF, dv)
        new_fl = logf.reshape(n_seq, t_new, H_FOX)
        act4 = act.reshape(n_seq, t_new, -1)
        fq = act4[..., _A_FQ * W_BR:(_A_FQ + 1) * W_BR]
        qf = jnp.where(head_eye, fq[:, None], 0).reshape(n_seq, H_FOX * t_new, W_BR)
        dq = act4[..., _A_DQ * W_BR:(_A_DQ + 1) * W_BR].reshape(n_seq, t_new, H_DIFF, dv)
        dq2 = jnp.stack([jnp.where(lane_lo, dq, 0), jnp.where(lane_lo, 0, dq)], axis=0)
        qd = jnp.transpose(dq2, (1, 3, 0, 2, 4)).reshape(n_seq, H_DIFF * 2 * t_new, dv)
        qm = jnp.swapaxes(act4[..., _A_MQ * W_BR:(_A_MQ + 1) * W_BR].reshape(n_seq, t_new, H_MEM, D_MEM), 1, 2)
        qm = qm.reshape(n_seq, H_MEM * t_new, D_MEM)
        nfl = jnp.pad(jnp.swapaxes(new_fl, 1, 2), ((0, 0), (0, 0), (0, page - t_new)))
        o_fox, o_diff, o_mem = _sample_attention(
            l, page_table, qf, qd, qm, caches,
            (kv32[:, _KV_FK].reshape(n_seq, t_new, W_BR), kv32[:, _KV_FV].reshape(n_seq, t_new, W_BR),
             nfl, new_dk, new_dv),
            cache_mem_k, cache_mem_v, slopes, lamv, g, lam_init)
        xs = _merge(xs, o_fox.reshape(m_s, W_BR), o_diff.reshape(m_s, W_BR), o_mem.reshape(m_s, W_BR),
                    act, merge_cols, *merge_w, l, alpha)
        outs[7].append(new_fk)
        outs[8].append(new_fv)
        outs[9].append(new_fl)
        outs[10].append(new_dk)
        outs[11].append(new_dv)

    stacked = [jnp.stack(o) if o else None for o in outs]
    stacked[0] = jnp.transpose(kv_stacks[_KV_FK].reshape(depth, b, H_FOX, D_FOX, t), (0, 1, 4, 2, 3))
    stacked[1] = jnp.transpose(kv_stacks[_KV_FV].reshape(depth, b, H_FOX, D_FOX, t), (0, 1, 4, 2, 3))
    stacked[3] = kv_stacks[_KV_DK].reshape(depth, b, t, H_DIFF, dv)
    stacked[4] = kv_stacks[_KV_DV].reshape(depth, b, t, H_DIFF, dv)
    return (xp.reshape(b, t, D_MODEL), xs.reshape(n_seq, t_new, D_MODEL)) + tuple(stacked)
```

```python
import functools
import math

import jax
import jax.numpy as jnp
from jax import lax
from jax.experimental import pallas as pl
from jax.experimental.pallas import tpu as pltpu

F32 = jnp.float32
BF16 = jnp.bfloat16

D_MODEL = 1024
H_FOX, D_FOX = 8, 64
H_DIFF, D_DIFF = 4, 64
H_MEM, D_MEM = 4, 128
W_BR = 512
N_BRANCH = 3
LN_EPS = 1e-5
RMS_EPS = 1e-5
LANES = 128
NEG = -1e30

VMEM_LIMIT = 48 * 1024 * 1024


def _log2(n):
    assert n & (n - 1) == 0, n
    return n.bit_length() - 1


def _cparams(sem):
    return pltpu.CompilerParams(dimension_semantics=sem, vmem_limit_bytes=VMEM_LIMIT)


def _mm_body(x_ref, w_ref, *o_refs):
    r = jnp.dot(x_ref[...].astype(BF16), w_ref[0], preferred_element_type=F32)
    for o in o_refs:
        o[...] = r.astype(o.dtype).reshape(o.shape)


def _matmul(x, w, l, tn, out_dtype, tm=512):
    m, k = x.shape
    n = w.shape[2]
    tm = min(tm, m)
    return pl.pallas_call(
        _mm_body,
        grid=(n // tn, m // tm),
        in_specs=[pl.BlockSpec((tm, k), lambda j, i: (i, 0)),
                  pl.BlockSpec((1, k, tn), lambda j, i: (l, 0, j))],
        out_specs=pl.BlockSpec((tm, tn), lambda j, i: (i, j)),
        out_shape=jax.ShapeDtypeStruct((m, n), out_dtype),
        name="proj",
        compiler_params=_cparams(("parallel", "parallel")),
    )(x, w)


def _matmul_dual(x, w, l, tn, tm=512):
    m, k = x.shape
    n = w.shape[2]
    tm = min(tm, m)
    return pl.pallas_call(
        _mm_body,
        grid=(n // tn, m // tm),
        in_specs=[pl.BlockSpec((tm, k), lambda j, i: (i, 0)),
                  pl.BlockSpec((1, k, tn), lambda j, i: (l, 0, j))],
        out_specs=[pl.BlockSpec((1, tm, tn), lambda j, i: (j, i, 0)),
                   pl.BlockSpec((tm, tn), lambda j, i: (i, j))],
        out_shape=[jax.ShapeDtypeStruct((n // tn, m, tn), F32),
                   jax.ShapeDtypeStruct((m, n), BF16)],
        name="proj_mem_kv",
        compiler_params=_cparams(("parallel", "parallel")),
    )(x, w)


def _kv_body(x_ref, w_ref, *refs, n_out, n_tr):
    o_refs = refs[-(n_out + 1):]
    r = jnp.dot(x_ref[...].astype(BF16), w_ref[0], preferred_element_type=F32)
    for n in range(n_out):
        rn = r[:, n * W_BR:(n + 1) * W_BR]
        if n < n_tr:
            o_refs[n][0, 0] = rn.T
        else:
            o_refs[n][0] = rn.reshape(o_refs[n].shape[1:])
    o_refs[n_out][...] = r.astype(BF16)


def _proj_kv(x, w, l, stacks, b, t, n_tr, tm=512):
    m, k = x.shape
    depth, _, n = w.shape
    n_out = n // W_BR
    tm = min(tm, t)
    per_b = t // tm
    in_specs = [pl.BlockSpec((tm, k), lambda i: (i, 0)),
                pl.BlockSpec((1, k, n), lambda i: (l, 0, 0))]
    operands = [x, w]
    aliases = {}
    if stacks is not None:
        in_specs += [pl.BlockSpec(memory_space=pl.ANY)] * n_out
        operands += list(stacks)
        aliases = {2 + i: i for i in range(n_out)}
    tr_spec = pl.BlockSpec((1, 1, W_BR, tm), lambda i: (l, i // per_b, 0, i % per_b))
    row_spec = pl.BlockSpec((1, tm, H_DIFF, 2 * D_DIFF), lambda i: (l, i, 0, 0))
    res = pl.pallas_call(
        functools.partial(_kv_body, n_out=n_out, n_tr=n_tr),
        grid=(m // tm,),
        in_specs=in_specs,
        out_specs=[tr_spec] * n_tr + [row_spec] * (n_out - n_tr) + [pl.BlockSpec((tm, n), lambda i: (i, 0))],
        out_shape=[jax.ShapeDtypeStruct((depth, b, W_BR, t), F32)] * n_tr
                  + [jax.ShapeDtypeStruct((depth, m, H_DIFF, 2 * D_DIFF), F32)] * (n_out - n_tr)
                  + [jax.ShapeDtypeStruct((m, n), BF16)],
        input_output_aliases=aliases,
        name="proj_kv",
        compiler_params=_cparams(("parallel",)),
    )(*operands)
    return tuple(res[:n_out]), res[n_out]


def _logf_body(x_ref, w_ref, b_ref, o_ref):
    z = jnp.dot(x_ref[...].astype(BF16), w_ref[0], preferred_element_type=F32) + b_ref[0]
    o_ref[...] = jnp.minimum(z, 0.0) - jnp.log1p(jnp.exp(-jnp.abs(z)))


def _logf_proj(x, w_pad, b_pad, l, tm=512):
    m, k = x.shape
    tm = min(tm, m)
    return pl.pallas_call(
        _logf_body,
        grid=(m // tm,),
        in_specs=[pl.BlockSpec((tm, k), lambda i: (i, 0)),
                  pl.BlockSpec((1, k, LANES), lambda i: (l, 0, 0)),
                  pl.BlockSpec((1, 1, LANES), lambda i: (l, 0, 0))],
        out_specs=pl.BlockSpec((tm, LANES), lambda i: (i, 0)),
        out_shape=jax.ShapeDtypeStruct((m, LANES), F32),
        name="proj_logf",
        compiler_params=_cparams(("parallel",)),
    )(x, w_pad, b_pad)


def _dot01(a, b01):
    hi = a.astype(BF16)
    r1 = a - hi.astype(F32)
    mid = r1.astype(BF16)
    lo = (r1 - mid.astype(F32)).astype(BF16)
    d = functools.partial(jnp.dot, preferred_element_type=F32)
    return d(hi, b01) + d(mid, b01) + d(lo, b01)


def _upper_ones(n):
    r = lax.broadcasted_iota(jnp.int32, (n, n), 0)
    c = lax.broadcasted_iota(jnp.int32, (n, n), 1)
    return jnp.where(r <= c, 1.0, 0.0).astype(BF16)


def _cumsum_body(x_ref, o_ref):
    x = x_ref[0]
    rows = x.shape[0]
    within = _dot01(x, _upper_ones(LANES))
    r = lax.broadcasted_iota(jnp.int32, (rows, rows), 0)
    c = lax.broadcasted_iota(jnp.int32, (rows, rows), 1)
    strict_lower = jnp.where(c < r, 1.0, 0.0).astype(BF16)
    x_hi = x.astype(BF16)
    r1 = x - x_hi.astype(F32)
    x_mid = r1.astype(BF16)
    x_lo = (r1 - x_mid.astype(F32)).astype(BF16)
    d = functools.partial(jnp.dot, preferred_element_type=F32)
    prev = d(strict_lower, x_hi) + d(strict_lower, x_mid) + d(strict_lower, x_lo)
    all_ones = jnp.ones((LANES, LANES), BF16)
    o_ref[0] = within + _dot01(prev, all_ones)


def _cumsum_seq(x):
    n, t = x.shape
    rows = t // LANES
    x3 = x.reshape(n, rows, LANES)
    out = pl.pallas_call(
        _cumsum_body,
        grid=(n,),
        in_specs=[pl.BlockSpec((1, rows, LANES), lambda i: (i, 0, 0))],
        out_specs=pl.BlockSpec((1, rows, LANES), lambda i: (i, 0, 0)),
        out_shape=jax.ShapeDtypeStruct((n, rows, LANES), F32),
        name="logf_cumsum",
        compiler_params=_cparams(("parallel",)),
    )(x3)
    return out.reshape(n, t)


def _nt_dot(a, b):
    return lax.dot_general(a, b, (((1,), (1,)), ((), ())), preferred_element_type=F32)


def _tile_lanes(x, n):
    return x if n == 1 else jnp.concatenate([x] * n, axis=1)


def _softmax_step(s, m_ref, acc_ref, v_chunk):
    m_old = m_ref[...]
    m_new = jnp.maximum(m_old, jnp.broadcast_to(jnp.max(s, axis=1, keepdims=True), m_old.shape))
    alpha = jnp.exp(m_old - m_new)
    p = jnp.exp(s - _tile_lanes(m_new, s.shape[1] // LANES)).astype(BF16)
    acc_ref[...] = (_tile_lanes(alpha, acc_ref.shape[-1] // LANES) * acc_ref[...]
                    + jnp.dot(p, v_chunk, preferred_element_type=F32))
    m_ref[...] = m_new


def _causal_sweep(i, tq, tk, n_streams, scores, consume, s_ref):
    per_q = tq // tk
    assert per_q % 2 == 0, "chunk c lives in half c % 2 of s_ref, kept static by pairing chunks"

    def stage(c, slot):
        for st in range(n_streams):
            s_ref[slot, st] = scores(c, st)

    def chunk_pair(jj, carry):
        c = 2 * jj
        stage(c + 1, 1)
        for st in range(n_streams):
            consume(c, st, s_ref[0, st], None)
        stage(c + 2, 0)
        for st in range(n_streams):
            consume(c + 1, st, s_ref[1, st], None)
        return carry

    stage(0, 0)
    lax.fori_loop(0, i * (per_q // 2), chunk_pair, 0)
    for r in range(per_q):
        c = i * per_q + r
        if r + 1 < per_q:
            stage(c + 1, (r + 1) % 2)
        for st in range(n_streams):
            consume(c, st, s_ref[r % 2, st, pl.ds(r * tk, tq - r * tk), :], r * tk)


def _causal_mask(s):
    row = lax.broadcasted_iota(jnp.int32, s.shape, 0)
    col = lax.broadcasted_iota(jnp.int32, s.shape, 1)
    return jnp.where(col <= row, s, NEG)


def _rows_from(ref, lead, row0):
    if row0 is None:
        return ref.at[lead]
    return ref.at[lead, pl.ds(row0, ref.shape[1] - row0)]


def _fox_body(q_ref, k_ref, v_ref, f_ref, o_ref, vaug_ref, acc_ref, m_ref, s_ref, *, tq, tk):
    i = pl.program_id(2)
    t_all = k_ref.shape[1]
    half = D_FOX

    @pl.when(i == 0)
    def _():
        for c in range(t_all // tk):
            sl = slice(c * tk, (c + 1) * tk)
            v = v_ref[0, sl, :].astype(F32)
            lane = lax.broadcasted_iota(jnp.int32, v.shape, 1)
            v0 = jnp.where(lane < half, v, jnp.where(lane == half, 1.0, 0.0))
            v1 = jnp.where(lane >= half, v, jnp.where(lane == 0, 1.0, 0.0))
            vaug_ref[0, sl, :] = v0.astype(BF16)
            vaug_ref[1, sl, :] = v1.astype(BF16)

    q = q_ref[0]
    qlane = lax.broadcasted_iota(jnp.int32, q.shape, 1)
    q_heads = [jnp.where((qlane >= half) == (hh == 1), q, jnp.zeros_like(q)) for hh in range(2)]
    m_ref[...] = jnp.full(m_ref.shape, NEG, F32)
    acc_ref[...] = jnp.zeros(acc_ref.shape, F32)

    def scores(c, hh):
        start = pl.multiple_of(c * tk, tk)
        return _nt_dot(q_heads[hh], k_ref[0, pl.ds(start, tk), :]) - f_ref[0, hh, pl.ds(c, 1), :]

    def consume(c, hh, s, row0):
        start = pl.multiple_of(c * tk, tk)
        if row0 is not None:
            s = _causal_mask(s)
        _softmax_step(s, _rows_from(m_ref, hh, row0), _rows_from(acc_ref, hh, row0),
                      vaug_ref[hh, pl.ds(start, tk), :])

    _causal_sweep(i, tq, tk, 2, scores, consume, s_ref)
    acc0 = acc_ref[0]
    acc1 = acc_ref[1]
    o0 = acc0 * (1.0 / acc0[:, half:half + 1])
    o1 = acc1 * (1.0 / acc1[:, 0:1])
    lane = lax.broadcasted_iota(jnp.int32, o0.shape, 1)
    o_ref[0] = jnp.where(lane < half, o0, o1).astype(o_ref.dtype)


def _fox_attention(qa, kva, fcum, b, t, q_col0, k_col0, v_col0, tq, tk):
    npair = H_FOX // 2
    return pl.pallas_call(
        functools.partial(_fox_body, tq=tq, tk=tk),
        grid=(b, npair, t // tq),
        in_specs=[pl.BlockSpec((1, tq, LANES), lambda bi, g, i: (bi, i, q_col0 + g)),
                  pl.BlockSpec((1, t, LANES), lambda bi, g, i: (bi, 0, k_col0 + g)),
                  pl.BlockSpec((1, t, LANES), lambda bi, g, i: (bi, 0, v_col0 + g)),
                  pl.BlockSpec((1, 2, t // tk, tk), lambda bi, g, i: (bi, g, 0, 0))],
        out_specs=pl.BlockSpec((1, tq, LANES), lambda bi, g, i: (bi, i, g)),
        out_shape=jax.ShapeDtypeStruct((b, t, W_BR), BF16),
        scratch_shapes=[pltpu.VMEM((2, t, LANES), BF16),
                        pltpu.VMEM((2, tq, LANES), F32),
                        pltpu.VMEM((2, tq, LANES), F32),
                        pltpu.VMEM((2, 2, tq, tk), F32)],
        name="fox_attn",
        compiler_params=_cparams(("parallel", "parallel", "arbitrary")),
    )(qa, kva, kva, fcum)


def _lambda_value(lam_ref, lam_init):
    lv = lam_ref[0]
    a = jnp.sum(lv[0:1] * lv[1:2], axis=1, keepdims=True)
    c = jnp.sum(lv[2:3] * lv[3:4], axis=1, keepdims=True)
    return jnp.exp(a) - jnp.exp(c) + lam_init


def _rms_scale(o, g, lam_init):
    ms = jnp.mean(o * o, axis=-1, keepdims=True)
    return o * lax.rsqrt(ms + RMS_EPS) * g * (1.0 - lam_init)


def _diff_body(slope_ref, q_ref, k_ref, v_ref, lam_ref, g_ref, o_ref,
               vaug_ref, acc_ref, m_ref, s_ref, *, tq, tk, lam_init):
    h = pl.program_id(1)
    i = pl.program_id(2)
    t_all = k_ref.shape[1]
    half = D_DIFF
    dv = 2 * D_DIFF

    @pl.when(i == 0)
    def _():
        for c in range(t_all // tk):
            sl = slice(c * tk, (c + 1) * tk)
            vaug_ref[sl, :dv] = v_ref[0, sl, :]
            lane = lax.broadcasted_iota(jnp.int32, (tk, LANES), 1)
            vaug_ref[sl, dv:] = jnp.where(lane == 0, 1.0, 0.0).astype(BF16)

    slope = slope_ref[h]
    q = q_ref[0]
    qlane = lax.broadcasted_iota(jnp.int32, q.shape, 1)
    q_maps = [jnp.where((qlane >= half) == (mm == 1), q, jnp.zeros_like(q)) for mm in range(2)]
    m_ref[...] = jnp.full(m_ref.shape, NEG, F32)
    acc_ref[...] = jnp.zeros(acc_ref.shape, F32)

    def scores(c, mm):
        start = pl.multiple_of(c * tk, tk)
        rel = lax.broadcasted_iota(jnp.int32, (1, tk), 1) + (c * tk - i * tq)
        return _nt_dot(q_maps[mm], k_ref[0, pl.ds(start, tk), :]) + slope * rel.astype(F32)

    def consume(c, mm, s, row0):
        start = pl.multiple_of(c * tk, tk)
        if row0 is not None:
            s = _causal_mask(s)
        _softmax_step(s, _rows_from(m_ref, mm, row0), _rows_from(acc_ref, mm, row0),
                      vaug_ref[pl.ds(start, tk), :])

    _causal_sweep(i, tq, tk, 2, scores, consume, s_ref)
    a0 = acc_ref[0]
    a1 = acc_ref[1]
    o0 = a0[:, :dv] * (1.0 / a0[:, dv:dv + 1])
    o1 = a1[:, :dv] * (1.0 / a1[:, dv:dv + 1])
    lam = _lambda_value(lam_ref, lam_init)
    o_ref[0] = _rms_scale(o0 - lam * o1, g_ref[0], lam_init).astype(o_ref.dtype)


def _diff_attention(qa, kva, slopes, lamv, g, l, b, t, q_col0, k_col0, v_col0, lam_init, tq, tk):
    dv = 2 * D_DIFF
    return pl.pallas_call(
        functools.partial(_diff_body, tq=tq, tk=tk, lam_init=lam_init),
        grid=(b, H_DIFF, t // tq),
        in_specs=[pl.BlockSpec(memory_space=pltpu.SMEM),
                  pl.BlockSpec((1, tq, LANES), lambda bi, h, i: (bi, i, q_col0 + h)),
                  pl.BlockSpec((1, t, LANES), lambda bi, h, i: (bi, 0, k_col0 + h)),
                  pl.BlockSpec((1, t, LANES), lambda bi, h, i: (bi, 0, v_col0 + h)),
                  pl.BlockSpec((1, 4, D_DIFF), lambda bi, h, i: (l, 0, 0)),
                  pl.BlockSpec((1, 1, dv), lambda bi, h, i: (l, 0, 0))],
        out_specs=pl.BlockSpec((1, tq, LANES), lambda bi, h, i: (bi, i, h)),
        out_shape=jax.ShapeDtypeStruct((b, t, W_BR), BF16),
        scratch_shapes=[pltpu.VMEM((t, dv + LANES), BF16),
                        pltpu.VMEM((2, tq, dv + LANES), F32),
                        pltpu.VMEM((2, tq, LANES), F32),
                        pltpu.VMEM((2, 2, tq, tk), F32)],
        name="diff_attn",
        compiler_params=_cparams(("parallel", "parallel", "arbitrary")),
    )(slopes, qa, kva, kva, lamv, g)


def _mem_body(q_ref, k_ref, v_ref, o_ref):
    scale = D_MEM ** -0.5
    for h in range(H_MEM):
        sl = slice(h * D_MEM, (h + 1) * D_MEM)
        s = _nt_dot(q_ref[0, :, sl], k_ref[0, :, sl]) * scale
        m = jnp.max(s, axis=1, keepdims=True)
        p = jnp.exp(s - m)
        l = jnp.sum(p, axis=1, keepdims=True)
        o = jnp.dot(p.astype(BF16), v_ref[0, :, sl], preferred_element_type=F32)
        o_ref[0, :, sl] = (o * (1.0 / l)).astype(o_ref.dtype)


def _mem_attention(qa, mkv, b, t, q_col0, tq=512):
    n_mem = mkv.shape[1]
    return pl.pallas_call(
        _mem_body,
        grid=(b, t // tq),
        in_specs=[pl.BlockSpec((1, tq, W_BR), lambda bi, i: (bi, i, q_col0)),
                  pl.BlockSpec((1, n_mem, W_BR), lambda bi, i: (bi, 0, 0)),
                  pl.BlockSpec((1, n_mem, W_BR), lambda bi, i: (bi, 0, 1))],
        out_specs=pl.BlockSpec((1, tq, W_BR), lambda bi, i: (bi, i, 0)),
        out_shape=jax.ShapeDtypeStruct((b, t, W_BR), BF16),
        name="mem_attn",
        compiler_params=_cparams(("parallel", "parallel")),
    )(qa, mkv, mkv)


def _silu_gate(o_ref, g_ref):
    g = g_ref[...].astype(F32)
    return (o_ref[...].astype(F32) * (g * jax.nn.sigmoid(g))).astype(BF16)


def _merge_body(x_ref, of_ref, od_ref, om_ref, fg_ref, dg_ref, mg_ref, gates_ref,
                wpf_ref, wpd_ref, wpm_ref, wo_ref, lng_ref, lnb_ref, y_ref, *, alpha):
    d = functools.partial(jnp.dot, preferred_element_type=F32)
    merged = None
    for n, (o_ref, g_ref, w_ref) in enumerate(((of_ref, fg_ref, wpf_ref),
                                               (od_ref, dg_ref, wpd_ref),
                                               (om_ref, mg_ref, wpm_ref))):
        gate = jax.nn.sigmoid(gates_ref[:, n * D_MODEL:(n + 1) * D_MODEL].astype(F32))
        term = gate * d(_silu_gate(o_ref, g_ref), w_ref[0])
        merged = term if merged is None else merged + term
    z = alpha * x_ref[...] + d(merged.astype(BF16), wo_ref[0])
    mu = jnp.mean(z, axis=-1, keepdims=True)
    zc = z - mu
    var = jnp.mean(zc * zc, axis=-1, keepdims=True)
    y_ref[...] = zc * lax.rsqrt(var + LN_EPS) * lng_ref[0] + lnb_ref[0]


def _merge(x, o_fox, o_diff, o_mem, proj, cols, wpf, wpd, wpm, wo, ln_g, ln_b, l, alpha, tm=512):
    m = x.shape[0]
    tm = min(tm, m)
    fg_c, dg_c, mg_c, gates_c = cols
    row = lambda i: (i, 0)
    layer = lambda i: (l, 0, 0)
    return pl.pallas_call(
        functools.partial(_merge_body, alpha=alpha),
        grid=(m // tm,),
        in_specs=[pl.BlockSpec((tm, D_MODEL), row),
                  pl.BlockSpec((tm, W_BR), row),
                  pl.BlockSpec((tm, W_BR), row),
                  pl.BlockSpec((tm, W_BR), row),
                  pl.BlockSpec((tm, W_BR), lambda i: (i, fg_c)),
                  pl.BlockSpec((tm, W_BR), lambda i: (i, dg_c)),
                  pl.BlockSpec((tm, W_BR), lambda i: (i, mg_c)),
                  pl.BlockSpec((tm, N_BRANCH * D_MODEL), lambda i: (i, gates_c)),
                  pl.BlockSpec((1, W_BR, D_MODEL), layer),
                  pl.BlockSpec((1, W_BR, D_MODEL), layer),
                  pl.BlockSpec((1, W_BR, D_MODEL), layer),
                  pl.BlockSpec((1, D_MODEL, D_MODEL), layer),
                  pl.BlockSpec((1, 1, D_MODEL), layer),
                  pl.BlockSpec((1, 1, D_MODEL), layer)],
        out_specs=pl.BlockSpec((tm, D_MODEL), row),
        out_shape=jax.ShapeDtypeStruct((m, D_MODEL), F32),
        name="merge",
        compiler_params=_cparams(("parallel",)),
    )(x, o_fox, o_diff, o_mem, proj, proj, proj, proj, wpf, wpd, wpm, wo, ln_g, ln_b)


PAGES_PER_STEP = 16


def _masked_softmax_step(s, valid, m_ref, l_ref):
    if valid is not None:
        s = jnp.where(valid, s, NEG)
    m_old = m_ref[...]
    m_new = jnp.maximum(m_old, jnp.broadcast_to(jnp.max(s, axis=1, keepdims=True), m_old.shape))
    alpha = jnp.exp(m_old - m_new)
    p = jnp.exp(s - _tile_lanes(m_new, s.shape[1] // LANES))
    if valid is not None:
        p = jnp.where(valid, p, 0.0)
    l_ref[...] = alpha * l_ref[...] + jnp.broadcast_to(jnp.sum(p, axis=1, keepdims=True), m_old.shape)
    m_ref[...] = m_new
    return alpha, p


def _sample_body(pt_ref, qf_ref, qd_ref, qm_ref, *refs, n_steps, pp, page, t_new, lam_init):
    del pt_ref
    cfk, cfv, cfl, cdk, cdv = (refs[n * pp:(n + 1) * pp] for n in range(5))
    (nfk_ref, nfv_ref, nfl_ref, ndk_ref, ndv_ref, mk_ref, mv_ref, slope_ref, lam_ref, g_ref,
     of_ref, od_ref, om_ref, mf_ref, lf_ref, af_ref, md_ref, ld_ref, ad_ref, fc_ref) = refs[5 * pp:]
    step_id = pl.program_id(1)
    dv = 2 * D_DIFF
    rows = H_FOX * t_new
    nd = page * H_DIFF
    n_mem = mk_ref.shape[2]

    def head_match(n_rows, n_cols, rows_per_head, n_head):
        r = lax.broadcasted_iota(jnp.int32, (n_rows, n_cols), 0)
        c = lax.broadcasted_iota(jnp.int32, (n_rows, n_cols), 1)
        return (c & (n_head - 1)) == (r >> _log2(rows_per_head)), r, c

    @pl.when(step_id == 0)
    def _():
        mf_ref[...] = jnp.full(mf_ref.shape, NEG, F32)
        md_ref[...] = jnp.full(md_ref.shape, NEG, F32)
        for r in (lf_ref, af_ref, ld_ref, ad_ref, fc_ref):
            r[...] = jnp.zeros(r.shape, F32)
        mk = mk_ref[0, 0].reshape(n_mem * H_MEM, D_MEM).astype(BF16)
        mv = mv_ref[0, 0].reshape(n_mem * H_MEM, D_MEM).astype(BF16)
        valid, _, _ = head_match(H_MEM * t_new, n_mem * H_MEM, t_new, H_MEM)
        s = jnp.where(valid, _nt_dot(qm_ref[0], mk) * (D_MEM ** -0.5), NEG)
        p = jnp.where(valid, jnp.exp(s - jnp.max(s, axis=1, keepdims=True)), 0.0)
        l = jnp.sum(p, axis=1, keepdims=True)
        o = jnp.dot(p.astype(BF16), mv, preferred_element_type=F32) * (1.0 / l)
        for h in range(H_MEM):
            om_ref[0, :, h * D_MEM:(h + 1) * D_MEM] = o[h * t_new:(h + 1) * t_new].astype(om_ref.dtype)

    def attend(pages, new):
        n_pg = len(pages)
        carry = fc_ref[...]
        fcums = []
        for pg in pages:
            fcums.append(carry + _dot01(pg["fl"], _upper_ones(page)))
            carry = jnp.broadcast_to(fcums[-1][:, page - 1:page], carry.shape)
        fc_ref[...] = carry
        fbias = [jnp.concatenate([jnp.broadcast_to(fc[h:h + 1, :], (t_new, page)) for h in range(H_FOX)], axis=0)
                 for fc in fcums]
        s = jnp.concatenate([pg["fox_scores"]() - fb for pg, fb in zip(pages, fbias)], axis=1)
        valid = None
        if new:
            r = lax.broadcasted_iota(jnp.int32, s.shape, 0)
            c = lax.broadcasted_iota(jnp.int32, s.shape, 1)
            valid = c <= (r & (t_new - 1))
        alpha, p = _masked_softmax_step(s, valid, mf_ref, lf_ref)
        parts = [pg["fox_pv"](p[:, n * page:(n + 1) * page].astype(BF16)) for n, pg in enumerate(pages)]
        af_ref[...] = _tile_lanes(alpha, W_BR // LANES) * af_ref[...] + functools.reduce(jnp.add, parts)
        valid, r, c = head_match(rows, n_pg * nd, 2 * t_new, H_DIFF)
        if new:
            valid = valid & ((c >> _log2(H_DIFF)) <= (r & (t_new - 1)))
        c1 = lax.broadcasted_iota(jnp.int32, (1, nd), 1)
        slope_row = jnp.zeros((1, nd), F32)
        for h in range(H_DIFF):
            slope_row = jnp.where((c1 & (H_DIFF - 1)) == h, slope_ref[h], slope_row)
        rel = c1 >> _log2(H_DIFF)
        s = jnp.concatenate([_nt_dot(qd_ref[0], pg["dk"]) + slope_row * (rel + pg["pos0"]).astype(F32)
                             for pg in pages], axis=1)
        alpha, p = _masked_softmax_step(s, valid, md_ref, ld_ref)
        parts = [jnp.dot(p[:, n * nd:(n + 1) * nd].astype(BF16), pg["dv"], preferred_element_type=F32)
                 for n, pg in enumerate(pages)]
        ad_ref[...] = alpha * ad_ref[...] + functools.reduce(jnp.add, parts)

    attend([dict(
        fox_scores=lambda r=r: jnp.dot(qf_ref[0], cfk[r][0, 0].reshape(W_BR, page).astype(BF16),
                                       preferred_element_type=F32),
        fox_pv=lambda p, r=r: _nt_dot(p, cfv[r][0, 0].reshape(W_BR, page).astype(BF16)),
        fl=cfl[r][0, 0],
        dk=cdk[r][0, 0].reshape(nd, dv).astype(BF16),
        dv=cdv[r][0, 0].reshape(nd, dv).astype(BF16),
        pos0=(step_id * pp + r - n_steps * pp) * page) for r in range(pp)], False)

    @pl.when(step_id == n_steps - 1)
    def _():
        def pad_rows(x, n):
            return jnp.concatenate([x, jnp.zeros((n - x.shape[0], x.shape[1]), x.dtype)], axis=0).astype(BF16)

        attend([dict(
            fox_scores=lambda: _nt_dot(qf_ref[0], pad_rows(nfk_ref[0], page)),
            fox_pv=lambda p: jnp.dot(p, pad_rows(nfv_ref[0], page), preferred_element_type=F32),
            fl=nfl_ref[0],
            dk=pad_rows(ndk_ref[0].reshape(t_new * H_DIFF, dv), nd),
            dv=pad_rows(ndv_ref[0].reshape(t_new * H_DIFF, dv), nd),
            pos0=0)], True)
        o3 = (af_ref[...] * (1.0 / _tile_lanes(lf_ref[...], W_BR // LANES))).reshape(H_FOX, t_new, W_BR)
        own = (lax.broadcasted_iota(jnp.int32, o3.shape, 0)
               == lax.broadcasted_iota(jnp.int32, o3.shape, 2) >> _log2(D_FOX))
        of_ref[0] = jnp.sum(jnp.where(own, o3, 0.0), axis=0).astype(of_ref.dtype)
        lam = _lambda_value(lam_ref, lam_init)
        o = ad_ref[...] * (1.0 / ld_ref[...])
        for h in range(H_DIFF):
            o1 = o[h * 2 * t_new:h * 2 * t_new + t_new]
            o2 = o[h * 2 * t_new + t_new:(h + 1) * 2 * t_new]
            od_ref[0, :, h * dv:(h + 1) * dv] = _rms_scale(o1 - lam * o2, g_ref[0], lam_init).astype(od_ref.dtype)


def _sample_attention(l, page_table, qf, qd, qm, caches, news, mem_k, mem_v, slopes, lamv, g, lam_init):
    cfk, cfv, cfl, cdk, cdv = caches
    n_seq, n_pages = page_table.shape
    page = cdk.shape[2]
    t_new = news[0].shape[1]
    n_mem = mem_k.shape[2]
    pp = PAGES_PER_STEP
    assert n_pages % pp == 0, (n_pages, pp)
    n_steps = n_pages // pp
    dv = 2 * D_DIFF

    def cache_spec(shape, r):
        return pl.BlockSpec((1, 1) + shape,
                            lambda s, p, pt, r=r: (l, pt[s * n_pages + p * pp + r]) + (0,) * len(shape))

    def seq_spec(shape):
        return pl.BlockSpec((1,) + shape, lambda s, p, pt: (s,) + (0,) * len(shape))

    layer3 = lambda s, p, pt: (l, 0, 0)
    mem_spec = pl.BlockSpec((1, 1, n_mem, H_MEM, D_MEM), lambda s, p, pt: (l, s, 0, 0, 0))
    rows = H_FOX * t_new
    in_specs = [seq_spec((rows, W_BR)), seq_spec((rows, dv)), seq_spec((H_MEM * t_new, D_MEM))]
    operands = [qf, qd, qm]
    for arr, shape in ((cfk, (H_FOX, D_FOX, page)), (cfv, (H_FOX, D_FOX, page)), (cfl, (H_FOX, page)),
                       (cdk, (page, H_DIFF, dv)), (cdv, (page, H_DIFF, dv))):
        for r in range(pp):
            in_specs.append(cache_spec(shape, r))
            operands.append(arr)
    in_specs += [seq_spec((t_new, W_BR)), seq_spec((t_new, W_BR)), seq_spec((H_FOX, page)),
                 seq_spec((t_new, H_DIFF, dv)), seq_spec((t_new, H_DIFF, dv)),
                 mem_spec, mem_spec,
                 pl.BlockSpec(memory_space=pltpu.SMEM),
                 pl.BlockSpec((1, 4, D_DIFF), layer3), pl.BlockSpec((1, 1, dv), layer3)]
    operands += list(news) + [mem_k, mem_v, slopes, lamv, g]
    grid_spec = pltpu.PrefetchScalarGridSpec(
        num_scalar_prefetch=1,
        grid=(n_seq, n_steps),
        in_specs=in_specs,
        out_specs=[seq_spec((t_new, W_BR)), seq_spec((t_new, W_BR)), seq_spec((t_new, W_BR))],
        scratch_shapes=[pltpu.VMEM((rows, LANES), F32), pltpu.VMEM((rows, LANES), F32),
                        pltpu.VMEM((rows, W_BR), F32),
                        pltpu.VMEM((rows, LANES), F32), pltpu.VMEM((rows, LANES), F32),
                        pltpu.VMEM((rows, dv), F32),
                        pltpu.VMEM((8, LANES), F32)])
    return pl.pallas_call(
        functools.partial(_sample_body, n_steps=n_steps, pp=pp, page=page, t_new=t_new, lam_init=lam_init),
        grid_spec=grid_spec,
        out_shape=[jax.ShapeDtypeStruct((n_seq, t_new, W_BR), BF16),
                   jax.ShapeDtypeStruct((n_seq, t_new, W_BR), BF16),
                   jax.ShapeDtypeStruct((n_seq, t_new, W_BR), BF16)],
        name="sample_attn",
        compiler_params=_cparams(("parallel", "arbitrary")),
    )(page_table.reshape(-1), *operands)


_A_FQ, _A_FG, _A_DQ, _A_DG, _A_MQ, _A_MG = range(6)
_A_GATES = 6 * W_BR
_KV_FK, _KV_FV, _KV_DK, _KV_DV = range(4)


def _split_w_in(w):
    sizes = (W_BR, W_BR, W_BR, W_BR, H_FOX, W_BR, W_BR, W_BR, W_BR, W_BR, W_BR, N_BRANCH * D_MODEL)
    offs = [0]
    for s in sizes:
        offs.append(offs[-1] + s)
    fq, fk, fv, fg, ff, dq, dk, dv, dg, mq, mg, gates = [w[..., offs[n]:offs[n + 1]] for n in range(12)]
    w_act = jnp.concatenate([fq * (D_FOX ** -0.5), fg, dq * (D_DIFF ** -0.5), dg, mq, mg, gates],
                            axis=-1).astype(BF16)
    w_kv = jnp.concatenate([fk, fv, dk, dv], axis=-1).astype(BF16)
    w_ff = jnp.pad(ff, ((0, 0), (0, 0), (0, LANES - H_FOX))).astype(BF16)
    return w_act, w_kv, w_ff


def kernel(x_prompt, x_sample, mem_prompt, cache_fox_k, cache_fox_v, cache_fox_logf, cache_diff_k,
           cache_diff_v, cache_mem_k, cache_mem_v, page_table, w_in, b_forget, w_mem_kv, lambda_q1,
           lambda_k1, lambda_q2, lambda_k2, diff_norm_g, w_proj_fox, w_proj_diff, w_proj_mem, w_out,
           ln_g, ln_b):
    depth = w_in.shape[0]
    b, t, _ = x_prompt.shape
    n_seq, t_new, _ = x_sample.shape
    page = cache_fox_k.shape[2]
    n_mem = mem_prompt.shape[1]
    alpha = (2 * depth) ** 0.25
    tq = min(1024, t)
    tk = min(512, t)
    dv = 2 * D_DIFF
    nblk = W_BR // LANES

    slopes = 2.0 ** (-8.0 * jnp.arange(1, H_DIFF + 1, dtype=F32) / H_DIFF)
    cfk = jnp.transpose(cache_fox_k, (0, 1, 3, 4, 2))
    cfv = jnp.transpose(cache_fox_v, (0, 1, 3, 4, 2))
    cfl = jnp.swapaxes(cache_fox_logf, 2, 3)
    caches = (cfk, cfv, cfl, cache_diff_k, cache_diff_v)
    lane_lo = (jnp.arange(dv) < D_DIFF)
    head_eye = (jnp.arange(H_FOX)[:, None, None] == jnp.arange(W_BR)[None, None, :] // D_FOX)

    w_act, w_kv, w_ff = _split_w_in(w_in)
    w_mkv = w_mem_kv.astype(BF16)
    b_ff = jnp.pad(b_forget, ((0, 0), (0, LANES - H_FOX))).reshape(depth, 1, LANES)
    lamv = jnp.stack([lambda_q1, lambda_k1, lambda_q2, lambda_k2], axis=1)
    g = diff_norm_g.reshape(depth, 1, dv)
    merge_w = (w_proj_fox.astype(BF16), w_proj_diff.astype(BF16), w_proj_mem.astype(BF16),
               w_out.astype(BF16), ln_g.reshape(depth, 1, D_MODEL), ln_b.reshape(depth, 1, D_MODEL))
    merge_cols = (_A_FG, _A_DG, _A_MG, _A_GATES // (N_BRANCH * D_MODEL))

    xp = x_prompt.reshape(b * t, D_MODEL)
    xs = x_sample.reshape(n_seq * t_new, D_MODEL)
    m_s = n_seq * t_new
    mem2 = mem_prompt.reshape(b * n_mem, D_MODEL)
    outs = [[] for _ in range(12)]
    kv_stacks = None

    for l in range(depth):
        lam_init = 0.8 - 0.6 * math.exp(-0.3 * l)

        act = _matmul(xp, w_act, l, 3072, BF16)
        kv_stacks, kv16 = _proj_kv(xp, w_kv, l, kv_stacks, b, t, 2)
        logf = _logf_proj(xp, w_ff, b_ff, l)[:, :H_FOX]
        mkv32, mkv16 = _matmul_dual(mem2, w_mkv, l, W_BR)
        fcum = _cumsum_seq(jnp.swapaxes(logf.reshape(b, t, H_FOX), 1, 2).reshape(b * H_FOX, t))
        act3 = act.reshape(b, t, -1)
        kv3 = kv16.reshape(b, t, -1)
        o_fox = _fox_attention(act3, kv3, fcum.reshape(b, H_FOX, t // tk, tk), b, t,
                               _A_FQ * nblk, _KV_FK * nblk, _KV_FV * nblk, tq, tk)
        o_diff = _diff_attention(act3, kv3, slopes, lamv, g, l, b, t,
                                 _A_DQ * nblk, _KV_DK * nblk, _KV_DV * nblk, lam_init, tq, tk)
        o_mem = _mem_attention(act3, mkv16.reshape(b, n_mem, 2 * W_BR), b, t, _A_MQ, tq=tk)
        xp = _merge(xp, o_fox.reshape(b * t, W_BR), o_diff.reshape(b * t, W_BR),
                    o_mem.reshape(b * t, W_BR), act, merge_cols, *merge_w, l, alpha)
        outs[2].append(logf.reshape(b, t, H_FOX))
        outs[5].append(mkv32[0].reshape(b, n_mem, H_MEM, D_MEM))
        outs[6].append(mkv32[1].reshape(b, n_mem, H_MEM, D_MEM))

        act = _matmul(xs, w_act, l, 3072, BF16)
        kv32 = _matmul(xs, w_kv, l, W_BR, F32).reshape(m_s, 4, W_BR)
        logf = _logf_proj(xs, w_ff, b_ff, l)[:, :H_FOX]
        new_fk = kv32[:, _KV_FK].reshape(n_seq, t_new, H_FOX, D_FOX)
        new_fv = kv32[:, _KV_FV].reshape(n_seq, t_new, H_FOX, D_FOX)
        new_dk = kv32[:, _KV_DK].reshape(n_seq, t_new, H_DIFF, dv)
        new_dv = kv32[:, _KV_DV].reshape(n_seq, t_new, H_DIFF, dv)
        new_fl = logf.reshape(n_seq, t_new, H_FOX)
        act4 = act.reshape(n_seq, t_new, -1)
        fq = act4[..., _A_FQ * W_BR:(_A_FQ + 1) * W_BR]
        qf = jnp.where(head_eye, fq[:, None], 0).reshape(n_seq, H_FOX * t_new, W_BR)
        dq = act4[..., _A_DQ * W_BR:(_A_DQ + 1) * W_BR].reshape(n_seq, t_new, H_DIFF, dv)
        dq2 = jnp.stack([jnp.where(lane_lo, dq, 0), jnp.where(lane_lo, 0, dq)], axis=0)
        qd = jnp.transpose(dq2, (1, 3, 0, 2, 4)).reshape(n_seq, H_DIFF * 2 * t_new, dv)
        qm = jnp.swapaxes(act4[..., _A_MQ * W_BR:(_A_MQ + 1) * W_BR].reshape(n_seq, t_new, H_MEM, D_MEM), 1, 2)
        qm = qm.reshape(n_seq, H_MEM * t_new, D_MEM)
        nfl = jnp.pad(jnp.swapaxes(new_fl, 1, 2), ((0, 0), (0, 0), (0, page - t_new)))
        o_fox, o_diff, o_mem = _sample_attention(
            l, page_table, qf, qd, qm, caches,
            (kv32[:, _KV_FK].reshape(n_seq, t_new, W_BR), kv32[:, _KV_FV].reshape(n_seq, t_new, W_BR),
             nfl, new_dk, new_dv),
            cache_mem_k, cache_mem_v, slopes, lamv, g, lam_init)
        xs = _merge(xs, o_fox.reshape(m_s, W_BR), o_diff.reshape(m_s, W_BR), o_mem.reshape(m_s, W_BR),
                    act, merge_cols, *merge_w, l, alpha)
        outs[7].append(new_fk)
        outs[8].append(new_fv)
        outs[9].append(new_fl)
        outs[10].append(new_dk)
        outs[11].append(new_dv)

    stacked = [jnp.stack(o) if o else None for o in outs]
    stacked[0] = jnp.transpose(kv_stacks[_KV_FK].reshape(depth, b, H_FOX, D_FOX, t), (0, 1, 4, 2, 3))
    stacked[1] = jnp.transpose(kv_stacks[_KV_FV].reshape(depth, b, H_FOX, D_FOX, t), (0, 1, 4, 2, 3))
    stacked[3] = kv_stacks[_KV_DK].reshape(depth, b, t, H_DIFF, dv)
    stacked[4] = kv_stacks[_KV_DV].reshape(depth, b, t, H_DIFF, dv)
    return (xp.reshape(b, t, D_MODEL), xs.reshape(n_seq, t_new, D_MODEL)) + tuple(stacked)
```

```python
import functools
import math

import jax
import jax.numpy as jnp
from jax import lax
from jax.experimental import pallas as pl
from jax.experimental.pallas import tpu as pltpu

F32 = jnp.float32
BF16 = jnp.bfloat16

D_MODEL = 1024
H_FOX, D_FOX = 8, 64
H_DIFF, D_DIFF = 4, 64
H_MEM, D_MEM = 4, 128
W_BR = 512
N_BRANCH = 3
LN_EPS = 1e-5
RMS_EPS = 1e-5
LANES = 128
NEG = -1e30

VMEM_LIMIT = 48 * 1024 * 1024


def _log2(n):
    assert n & (n - 1) == 0, n
    return n.bit_length() - 1


def _cparams(sem):
    return pltpu.CompilerParams(dimension_semantics=sem, vmem_limit_bytes=VMEM_LIMIT)


def _mm_body(x_ref, w_ref, *o_refs):
    r = jnp.dot(x_ref[...].astype(BF16), w_ref[0], preferred_element_type=F32)
    for o in o_refs:
        o[...] = r.astype(o.dtype).reshape(o.shape)


def _matmul(x, w, l, tn, out_dtype, tm=512):
    m, k = x.shape
    n = w.shape[2]
    tm = min(tm, m)
    return pl.pallas_call(
        _mm_body,
        grid=(n // tn, m // tm),
        in_specs=[pl.BlockSpec((tm, k), lambda j, i: (i, 0)),
                  pl.BlockSpec((1, k, tn), lambda j, i: (l, 0, j))],
        out_specs=pl.BlockSpec((tm, tn), lambda j, i: (i, j)),
        out_shape=jax.ShapeDtypeStruct((m, n), out_dtype),
        name="proj",
        compiler_params=_cparams(("parallel", "parallel")),
    )(x, w)


def _matmul_dual(x, w, l, tn, tm=512):
    m, k = x.shape
    n = w.shape[2]
    tm = min(tm, m)
    return pl.pallas_call(
        _mm_body,
        grid=(n // tn, m // tm),
        in_specs=[pl.BlockSpec((tm, k), lambda j, i: (i, 0)),
                  pl.BlockSpec((1, k, tn), lambda j, i: (l, 0, j))],
        out_specs=[pl.BlockSpec((1, tm, tn), lambda j, i: (j, i, 0)),
                   pl.BlockSpec((tm, tn), lambda j, i: (i, j))],
        out_shape=[jax.ShapeDtypeStruct((n // tn, m, tn), F32),
                   jax.ShapeDtypeStruct((m, n), BF16)],
        name="proj_mem_kv",
        compiler_params=_cparams(("parallel", "parallel")),
    )(x, w)


def _kv_body(x_ref, w_ref, *refs, n_out, n_tr):
    o_refs = refs[-(n_out + 1):]
    r = jnp.dot(x_ref[...].astype(BF16), w_ref[0], preferred_element_type=F32)
    for n in range(n_out):
        rn = r[:, n * W_BR:(n + 1) * W_BR]
        if n < n_tr:
            o_refs[n][0, 0] = rn.T
        else:
            o_refs[n][0] = rn.reshape(o_refs[n].shape[1:])
    o_refs[n_out][...] = r.astype(BF16)


def _proj_kv(x, w, l, stacks, b, t, n_tr, tm=512):
    m, k = x.shape
    depth, _, n = w.shape
    n_out = n // W_BR
    tm = min(tm, t)
    per_b = t // tm
    in_specs = [pl.BlockSpec((tm, k), lambda i: (i, 0)),
                pl.BlockSpec((1, k, n), lambda i: (l, 0, 0))]
    operands = [x, w]
    aliases = {}
    if stacks is not None:
        in_specs += [pl.BlockSpec(memory_space=pl.ANY)] * n_out
        operands += list(stacks)
        aliases = {2 + i: i for i in range(n_out)}
    tr_spec = pl.BlockSpec((1, 1, W_BR, tm), lambda i: (l, i // per_b, 0, i % per_b))
    row_spec = pl.BlockSpec((1, tm, H_DIFF, 2 * D_DIFF), lambda i: (l, i, 0, 0))
    res = pl.pallas_call(
        functools.partial(_kv_body, n_out=n_out, n_tr=n_tr),
        grid=(m // tm,),
        in_specs=in_specs,
        out_specs=[tr_spec] * n_tr + [row_spec] * (n_out - n_tr) + [pl.BlockSpec((tm, n), lambda i: (i, 0))],
        out_shape=[jax.ShapeDtypeStruct((depth, b, W_BR, t), F32)] * n_tr
                  + [jax.ShapeDtypeStruct((depth, m, H_DIFF, 2 * D_DIFF), F32)] * (n_out - n_tr)
                  + [jax.ShapeDtypeStruct((m, n), BF16)],
        input_output_aliases=aliases,
        name="proj_kv",
        compiler_params=_cparams(("parallel",)),
    )(*operands)
    return tuple(res[:n_out]), res[n_out]


def _logf_body(x_ref, w_ref, b_ref, o_ref):
    z = jnp.dot(x_ref[...].astype(BF16), w_ref[0], preferred_element_type=F32) + b_ref[0]
    o_ref[...] = jnp.minimum(z, 0.0) - jnp.log1p(jnp.exp(-jnp.abs(z)))


def _logf_proj(x, w_pad, b_pad, l, tm=512):
    m, k = x.shape
    tm = min(tm, m)
    return pl.pallas_call(
        _logf_body,
        grid=(m // tm,),
        in_specs=[pl.BlockSpec((tm, k), lambda i: (i, 0)),
                  pl.BlockSpec((1, k, LANES), lambda i: (l, 0, 0)),
                  pl.BlockSpec((1, 1, LANES), lambda i: (l, 0, 0))],
        out_specs=pl.BlockSpec((tm, LANES), lambda i: (i, 0)),
        out_shape=jax.ShapeDtypeStruct((m, LANES), F32),
        name="proj_logf",
        compiler_params=_cparams(("parallel",)),
    )(x, w_pad, b_pad)


def _dot01(a, b01):
    hi = a.astype(BF16)
    r1 = a - hi.astype(F32)
    mid = r1.astype(BF16)
    lo = (r1 - mid.astype(F32)).astype(BF16)
    d = functools.partial(jnp.dot, preferred_element_type=F32)
    return d(hi, b01) + d(mid, b01) + d(lo, b01)


def _upper_ones(n):
    r = lax.broadcasted_iota(jnp.int32, (n, n), 0)
    c = lax.broadcasted_iota(jnp.int32, (n, n), 1)
    return jnp.where(r <= c, 1.0, 0.0).astype(BF16)


def _cumsum_body(x_ref, o_ref):
    x = x_ref[0]
    rows = x.shape[0]
    within = _dot01(x, _upper_ones(LANES))
    r = lax.broadcasted_iota(jnp.int32, (rows, rows), 0)
    c = lax.broadcasted_iota(jnp.int32, (rows, rows), 1)
    strict_lower = jnp.where(c < r, 1.0, 0.0).astype(BF16)
    x_hi = x.astype(BF16)
    r1 = x - x_hi.astype(F32)
    x_mid = r1.astype(BF16)
    x_lo = (r1 - x_mid.astype(F32)).astype(BF16)
    d = functools.partial(jnp.dot, preferred_element_type=F32)
    prev = d(strict_lower, x_hi) + d(strict_lower, x_mid) + d(strict_lower, x_lo)
    all_ones = jnp.ones((LANES, LANES), BF16)
    o_ref[0] = within + _dot01(prev, all_ones)


def _cumsum_seq(x):
    n, t = x.shape
    rows = t // LANES
    x3 = x.reshape(n, rows, LANES)
    out = pl.pallas_call(
        _cumsum_body,
        grid=(n,),
        in_specs=[pl.BlockSpec((1, rows, LANES), lambda i: (i, 0, 0))],
        out_specs=pl.BlockSpec((1, rows, LANES), lambda i: (i, 0, 0)),
        out_shape=jax.ShapeDtypeStruct((n, rows, LANES), F32),
        name="logf_cumsum",
        compiler_params=_cparams(("parallel",)),
    )(x3)
    return out.reshape(n, t)


def _nt_dot(a, b):
    return lax.dot_general(a, b, (((1,), (1,)), ((), ())), preferred_element_type=F32)


def _tile_lanes(x, n):
    return x if n == 1 else jnp.concatenate([x] * n, axis=1)


def _softmax_step(s, m_ref, acc_ref, v_chunk):
    m_old = m_ref[...]
    m_new = jnp.maximum(m_old, jnp.broadcast_to(jnp.max(s, axis=1, keepdims=True), m_old.shape))
    alpha = jnp.exp(m_old - m_new)
    p = jnp.exp(s - _tile_lanes(m_new, s.shape[1] // LANES)).astype(BF16)
    acc_ref[...] = (_tile_lanes(alpha, acc_ref.shape[-1] // LANES) * acc_ref[...]
                    + jnp.dot(p, v_chunk, preferred_element_type=F32))
    m_ref[...] = m_new


def _causal_sweep(i, tq, tk, n_streams, scores, consume, s_ref):
    per_q = tq // tk
    assert per_q % 2 == 0, "chunk c lives in half c % 2 of s_ref, kept static by pairing chunks"

    def stage(c, slot):
        for st in range(n_streams):
            s_ref[slot, st] = scores(c, st)

    def chunk_pair(jj, carry):
        c = 2 * jj
        stage(c + 1, 1)
        for st in range(n_streams):
            consume(c, st, s_ref[0, st], None)
        stage(c + 2, 0)
        for st in range(n_streams):
            consume(c + 1, st, s_ref[1, st], None)
        return carry

    stage(0, 0)
    lax.fori_loop(0, i * (per_q // 2), chunk_pair, 0)
    for r in range(per_q):
        c = i * per_q + r
        if r + 1 < per_q:
            stage(c + 1, (r + 1) % 2)
        for st in range(n_streams):
            consume(c, st, s_ref[r % 2, st, pl.ds(r * tk, tq - r * tk), :], r * tk)


def _causal_sweep_direct(i, tq, tk, n_streams, scores, consume):
    per_q = tq // tk

    def full_chunk(j, carry):
        for st in range(n_streams):
            consume(j, st, scores(j, st), None)
        return carry

    lax.fori_loop(0, i * per_q, full_chunk, 0)
    for r in range(per_q):
        c = i * per_q + r
        for st in range(n_streams):
            consume(c, st, scores(c, st)[r * tk:], r * tk)


def _causal_mask(s):
    row = lax.broadcasted_iota(jnp.int32, s.shape, 0)
    col = lax.broadcasted_iota(jnp.int32, s.shape, 1)
    return jnp.where(col <= row, s, NEG)


def _rows_from(ref, lead, row0):
    if row0 is None:
        return ref.at[lead]
    return ref.at[lead, pl.ds(row0, ref.shape[1] - row0)]


def _fox_body(q_ref, k_ref, v_ref, f_ref, o_ref, vaug_ref, acc_ref, m_ref, *, tq, tk):
    i = pl.program_id(2)
    t_all = k_ref.shape[1]
    half = D_FOX

    @pl.when(i == 0)
    def _():
        for c in range(t_all // tk):
            sl = slice(c * tk, (c + 1) * tk)
            v = v_ref[0, sl, :].astype(F32)
            lane = lax.broadcasted_iota(jnp.int32, v.shape, 1)
            v0 = jnp.where(lane < half, v, jnp.where(lane == half, 1.0, 0.0))
            v1 = jnp.where(lane >= half, v, jnp.where(lane == 0, 1.0, 0.0))
            vaug_ref[0, sl, :] = v0.astype(BF16)
            vaug_ref[1, sl, :] = v1.astype(BF16)

    q = q_ref[0]
    qlane = lax.broadcasted_iota(jnp.int32, q.shape, 1)
    q_heads = [jnp.where((qlane >= half) == (hh == 1), q, jnp.zeros_like(q)) for hh in range(2)]
    m_ref[...] = jnp.full(m_ref.shape, NEG, F32)
    acc_ref[...] = jnp.zeros(acc_ref.shape, F32)

    def scores(c, hh):
        start = pl.multiple_of(c * tk, tk)
        return _nt_dot(q_heads[hh], k_ref[0, pl.ds(start, tk), :]) - f_ref[0, hh, pl.ds(c, 1), :]

    def consume(c, hh, s, row0):
        start = pl.multiple_of(c * tk, tk)
        if row0 is not None:
            s = _causal_mask(s)
        _softmax_step(s, _rows_from(m_ref, hh, row0), _rows_from(acc_ref, hh, row0),
                      vaug_ref[hh, pl.ds(start, tk), :])

    _causal_sweep_direct(i, tq, tk, 2, scores, consume)
    acc0 = acc_ref[0]
    acc1 = acc_ref[1]
    o0 = acc0 * (1.0 / acc0[:, half:half + 1])
    o1 = acc1 * (1.0 / acc1[:, 0:1])
    lane = lax.broadcasted_iota(jnp.int32, o0.shape, 1)
    o_ref[0] = jnp.where(lane < half, o0, o1).astype(o_ref.dtype)


def _fox_attention(qa, kva, fcum, b, t, q_col0, k_col0, v_col0, tq, tk):
    npair = H_FOX // 2
    return pl.pallas_call(
        functools.partial(_fox_body, tq=tq, tk=tk),
        grid=(b, npair, t // tq),
        in_specs=[pl.BlockSpec((1, tq, LANES), lambda bi, g, i: (bi, i, q_col0 + g)),
                  pl.BlockSpec((1, t, LANES), lambda bi, g, i: (bi, 0, k_col0 + g)),
                  pl.BlockSpec((1, t, LANES), lambda bi, g, i: (bi, 0, v_col0 + g)),
                  pl.BlockSpec((1, 2, t // tk, tk), lambda bi, g, i: (bi, g, 0, 0))],
        out_specs=pl.BlockSpec((1, tq, LANES), lambda bi, g, i: (bi, i, g)),
        out_shape=jax.ShapeDtypeStruct((b, t, W_BR), BF16),
        scratch_shapes=[pltpu.VMEM((2, t, LANES), BF16),
                        pltpu.VMEM((2, tq, LANES), F32),
                        pltpu.VMEM((2, tq, LANES), F32)],
        name="fox_attn",
        compiler_params=_cparams(("parallel", "parallel", "arbitrary")),
    )(qa, kva, kva, fcum)


def _lambda_value(lam_ref, lam_init):
    lv = lam_ref[0]
    a = jnp.sum(lv[0:1] * lv[1:2], axis=1, keepdims=True)
    c = jnp.sum(lv[2:3] * lv[3:4], axis=1, keepdims=True)
    return jnp.exp(a) - jnp.exp(c) + lam_init


def _rms_scale(o, g, lam_init):
    ms = jnp.mean(o * o, axis=-1, keepdims=True)
    return o * lax.rsqrt(ms + RMS_EPS) * g * (1.0 - lam_init)


def _diff_body(slope_ref, q_ref, k_ref, v_ref, lam_ref, g_ref, o_ref,
               vaug_ref, acc_ref, m_ref, s_ref, *, tq, tk, lam_init):
    h = pl.program_id(1)
    i = pl.program_id(2)
    t_all = k_ref.shape[1]
    half = D_DIFF
    dv = 2 * D_DIFF

    @pl.when(i == 0)
    def _():
        for c in range(t_all // tk):
            sl = slice(c * tk, (c + 1) * tk)
            vaug_ref[sl, :dv] = v_ref[0, sl, :]
            lane = lax.broadcasted_iota(jnp.int32, (tk, LANES), 1)
            vaug_ref[sl, dv:] = jnp.where(lane == 0, 1.0, 0.0).astype(BF16)

    slope = slope_ref[h]
    q = q_ref[0]
    qlane = lax.broadcasted_iota(jnp.int32, q.shape, 1)
    q_maps = [jnp.where((qlane >= half) == (mm == 1), q, jnp.zeros_like(q)) for mm in range(2)]
    m_ref[...] = jnp.full(m_ref.shape, NEG, F32)
    acc_ref[...] = jnp.zeros(acc_ref.shape, F32)

    def scores(c, mm):
        start = pl.multiple_of(c * tk, tk)
        rel = lax.broadcasted_iota(jnp.int32, (1, tk), 1) + (c * tk - i * tq)
        return _nt_dot(q_maps[mm], k_ref[0, pl.ds(start, tk), :]) + slope * rel.astype(F32)

    def consume(c, mm, s, row0):
        start = pl.multiple_of(c * tk, tk)
        if row0 is not None:
            s = _causal_mask(s)
        _softmax_step(s, _rows_from(m_ref, mm, row0), _rows_from(acc_ref, mm, row0),
                      vaug_ref[pl.ds(start, tk), :])

    _causal_sweep(i, tq, tk, 2, scores, consume, s_ref)
    a0 = acc_ref[0]
    a1 = acc_ref[1]
    o0 = a0[:, :dv] * (1.0 / a0[:, dv:dv + 1])
    o1 = a1[:, :dv] * (1.0 / a1[:, dv:dv + 1])
    lam = _lambda_value(lam_ref, lam_init)
    o_ref[0] = _rms_scale(o0 - lam * o1, g_ref[0], lam_init).astype(o_ref.dtype)


def _diff_attention(qa, kva, slopes, lamv, g, l, b, t, q_col0, k_col0, v_col0, lam_init, tq, tk):
    dv = 2 * D_DIFF
    return pl.pallas_call(
        functools.partial(_diff_body, tq=tq, tk=tk, lam_init=lam_init),
        grid=(b, H_DIFF, t // tq),
        in_specs=[pl.BlockSpec(memory_space=pltpu.SMEM),
                  pl.BlockSpec((1, tq, LANES), lambda bi, h, i: (bi, i, q_col0 + h)),
                  pl.BlockSpec((1, t, LANES), lambda bi, h, i: (bi, 0, k_col0 + h)),
                  pl.BlockSpec((1, t, LANES), lambda bi, h, i: (bi, 0, v_col0 + h)),
                  pl.BlockSpec((1, 4, D_DIFF), lambda bi, h, i: (l, 0, 0)),
                  pl.BlockSpec((1, 1, dv), lambda bi, h, i: (l, 0, 0))],
        out_specs=pl.BlockSpec((1, tq, LANES), lambda bi, h, i: (bi, i, h)),
        out_shape=jax.ShapeDtypeStruct((b, t, W_BR), BF16),
        scratch_shapes=[pltpu.VMEM((t, dv + LANES), BF16),
                        pltpu.VMEM((2, tq, dv + LANES), F32),
                        pltpu.VMEM((2, tq, LANES), F32),
                        pltpu.VMEM((2, 2, tq, tk), F32)],
        name="diff_attn",
        compiler_params=_cparams(("parallel", "parallel", "arbitrary")),
    )(slopes, qa, kva, kva, lamv, g)


def _mem_body(q_ref, k_ref, v_ref, o_ref):
    scale = D_MEM ** -0.5
    for h in range(H_MEM):
        sl = slice(h * D_MEM, (h + 1) * D_MEM)
        s = _nt_dot(q_ref[0, :, sl], k_ref[0, :, sl]) * scale
        m = jnp.max(s, axis=1, keepdims=True)
        p = jnp.exp(s - m)
        l = jnp.sum(p, axis=1, keepdims=True)
        o = jnp.dot(p.astype(BF16), v_ref[0, :, sl], preferred_element_type=F32)
        o_ref[0, :, sl] = (o * (1.0 / l)).astype(o_ref.dtype)


def _mem_attention(qa, mkv, b, t, q_col0, tq=512):
    n_mem = mkv.shape[1]
    return pl.pallas_call(
        _mem_body,
        grid=(b, t // tq),
        in_specs=[pl.BlockSpec((1, tq, W_BR), lambda bi, i: (bi, i, q_col0)),
                  pl.BlockSpec((1, n_mem, W_BR), lambda bi, i: (bi, 0, 0)),
                  pl.BlockSpec((1, n_mem, W_BR), lambda bi, i: (bi, 0, 1))],
        out_specs=pl.BlockSpec((1, tq, W_BR), lambda bi, i: (bi, i, 0)),
        out_shape=jax.ShapeDtypeStruct((b, t, W_BR), BF16),
        name="mem_attn",
        compiler_params=_cparams(("parallel", "parallel")),
    )(qa, mkv, mkv)


def _silu_gate(o_ref, g_ref):
    g = g_ref[...].astype(F32)
    return (o_ref[...].astype(F32) * (g * jax.nn.sigmoid(g))).astype(BF16)


def _merge_body(x_ref, of_ref, od_ref, om_ref, fg_ref, dg_ref, mg_ref, gates_ref,
                wpf_ref, wpd_ref, wpm_ref, wo_ref, lng_ref, lnb_ref, y_ref, *, alpha):
    d = functools.partial(jnp.dot, preferred_element_type=F32)
    merged = None
    for n, (o_ref, g_ref, w_ref) in enumerate(((of_ref, fg_ref, wpf_ref),
                                               (od_ref, dg_ref, wpd_ref),
                                               (om_ref, mg_ref, wpm_ref))):
        gate = jax.nn.sigmoid(gates_ref[:, n * D_MODEL:(n + 1) * D_MODEL].astype(F32))
        term = gate * d(_silu_gate(o_ref, g_ref), w_ref[0])
        merged = term if merged is None else merged + term
    z = alpha * x_ref[...] + d(merged.astype(BF16), wo_ref[0])
    mu = jnp.mean(z, axis=-1, keepdims=True)
    zc = z - mu
    var = jnp.mean(zc * zc, axis=-1, keepdims=True)
    y_ref[...] = zc * lax.rsqrt(var + LN_EPS) * lng_ref[0] + lnb_ref[0]


def _merge(x, o_fox, o_diff, o_mem, proj, cols, wpf, wpd, wpm, wo, ln_g, ln_b, l, alpha, tm=512):
    m = x.shape[0]
    tm = min(tm, m)
    fg_c, dg_c, mg_c, gates_c = cols
    row = lambda i: (i, 0)
    layer = lambda i: (l, 0, 0)
    return pl.pallas_call(
        functools.partial(_merge_body, alpha=alpha),
        grid=(m // tm,),
        in_specs=[pl.BlockSpec((tm, D_MODEL), row),
                  pl.BlockSpec((tm, W_BR), row),
                  pl.BlockSpec((tm, W_BR), row),
                  pl.BlockSpec((tm, W_BR), row),
                  pl.BlockSpec((tm, W_BR), lambda i: (i, fg_c)),
                  pl.BlockSpec((tm, W_BR), lambda i: (i, dg_c)),
                  pl.BlockSpec((tm, W_BR), lambda i: (i, mg_c)),
                  pl.BlockSpec((tm, N_BRANCH * D_MODEL), lambda i: (i, gates_c)),
                  pl.BlockSpec((1, W_BR, D_MODEL), layer),
                  pl.BlockSpec((1, W_BR, D_MODEL), layer),
                  pl.BlockSpec((1, W_BR, D_MODEL), layer),
                  pl.BlockSpec((1, D_MODEL, D_MODEL), layer),
                  pl.BlockSpec((1, 1, D_MODEL), layer),
                  pl.BlockSpec((1, 1, D_MODEL), layer)],
        out_specs=pl.BlockSpec((tm, D_MODEL), row),
        out_shape=jax.ShapeDtypeStruct((m, D_MODEL), F32),
        name="merge",
        compiler_params=_cparams(("parallel",)),
    )(x, o_fox, o_diff, o_mem, proj, proj, proj, proj, wpf, wpd, wpm, wo, ln_g, ln_b)


PAGES_PER_STEP = 16


def _masked_softmax_step(s, valid, m_ref, l_ref):
    if valid is not None:
        s = jnp.where(valid, s, NEG)
    m_old = m_ref[...]
    m_new = jnp.maximum(m_old, jnp.broadcast_to(jnp.max(s, axis=1, keepdims=True), m_old.shape))
    alpha = jnp.exp(m_old - m_new)
    p = jnp.exp(s - _tile_lanes(m_new, s.shape[1] // LANES))
    if valid is not None:
        p = jnp.where(valid, p, 0.0)
    l_ref[...] = alpha * l_ref[...] + jnp.broadcast_to(jnp.sum(p, axis=1, keepdims=True), m_old.shape)
    m_ref[...] = m_new
    return alpha, p


def _sample_body(pt_ref, qf_ref, qd_ref, qm_ref, *refs, n_steps, pp, page, t_new, lam_init):
    del pt_ref
    cfk, cfv, cfl, cdk, cdv = (refs[n * pp:(n + 1) * pp] for n in range(5))
    (nfk_ref, nfv_ref, nfl_ref, ndk_ref, ndv_ref, mk_ref, mv_ref, slope_ref, lam_ref, g_ref,
     of_ref, od_ref, om_ref, mf_ref, lf_ref, af_ref, md_ref, ld_ref, ad_ref, fc_ref) = refs[5 * pp:]
    step_id = pl.program_id(1)
    dv = 2 * D_DIFF
    rows = H_FOX * t_new
    nd = page * H_DIFF
    n_mem = mk_ref.shape[2]

    def head_match(n_rows, n_cols, rows_per_head, n_head):
        r = lax.broadcasted_iota(jnp.int32, (n_rows, n_cols), 0)
        c = lax.broadcasted_iota(jnp.int32, (n_rows, n_cols), 1)
        return (c & (n_head - 1)) == (r >> _log2(rows_per_head)), r, c

    @pl.when(step_id == 0)
    def _():
        mf_ref[...] = jnp.full(mf_ref.shape, NEG, F32)
        md_ref[...] = jnp.full(md_ref.shape, NEG, F32)
        for r in (lf_ref, af_ref, ld_ref, ad_ref, fc_ref):
            r[...] = jnp.zeros(r.shape, F32)
        mk = mk_ref[0, 0].reshape(n_mem * H_MEM, D_MEM).astype(BF16)
        mv = mv_ref[0, 0].reshape(n_mem * H_MEM, D_MEM).astype(BF16)
        valid, _, _ = head_match(H_MEM * t_new, n_mem * H_MEM, t_new, H_MEM)
        s = jnp.where(valid, _nt_dot(qm_ref[0], mk) * (D_MEM ** -0.5), NEG)
        p = jnp.where(valid, jnp.exp(s - jnp.max(s, axis=1, keepdims=True)), 0.0)
        l = jnp.sum(p, axis=1, keepdims=True)
        o = jnp.dot(p.astype(BF16), mv, preferred_element_type=F32) * (1.0 / l)
        for h in range(H_MEM):
            om_ref[0, :, h * D_MEM:(h + 1) * D_MEM] = o[h * t_new:(h + 1) * t_new].astype(om_ref.dtype)

    def attend(pages, new):
        n_pg = len(pages)
        carry = fc_ref[...]
        fcums = []
        for pg in pages:
            fcums.append(carry + _dot01(pg["fl"], _upper_ones(page)))
            carry = jnp.broadcast_to(fcums[-1][:, page - 1:page], carry.shape)
        fc_ref[...] = carry
        fbias = [jnp.concatenate([jnp.broadcast_to(fc[h:h + 1, :], (t_new, page)) for h in range(H_FOX)], axis=0)
                 for fc in fcums]
        s = jnp.concatenate([pg["fox_scores"]() - fb for pg, fb in zip(pages, fbias)], axis=1)
        valid = None
        if new:
            r = lax.broadcasted_iota(jnp.int32, s.shape, 0)
            c = lax.broadcasted_iota(jnp.int32, s.shape, 1)
            valid = c <= (r & (t_new - 1))
        alpha, p = _masked_softmax_step(s, valid, mf_ref, lf_ref)
        parts = [pg["fox_pv"](p[:, n * page:(n + 1) * page].astype(BF16)) for n, pg in enumerate(pages)]
        af_ref[...] = _tile_lanes(alpha, W_BR // LANES) * af_ref[...] + functools.reduce(jnp.add, parts)
        valid, r, c = head_match(rows, n_pg * nd, 2 * t_new, H_DIFF)
        if new:
            valid = valid & ((c >> _log2(H_DIFF)) <= (r & (t_new - 1)))
        c1 = lax.broadcasted_iota(jnp.int32, (1, nd), 1)
        slope_row = jnp.zeros((1, nd), F32)
        for h in range(H_DIFF):
            slope_row = jnp.where((c1 & (H_DIFF - 1)) == h, slope_ref[h], slope_row)
        rel = c1 >> _log2(H_DIFF)
        s = jnp.concatenate([_nt_dot(qd_ref[0], pg["dk"]) + slope_row * (rel + pg["pos0"]).astype(F32)
                             for pg in pages], axis=1)
        alpha, p = _masked_softmax_step(s, valid, md_ref, ld_ref)
        parts = [jnp.dot(p[:, n * nd:(n + 1) * nd].astype(BF16), pg["dv"], preferred_element_type=F32)
                 for n, pg in enumerate(pages)]
        ad_ref[...] = alpha * ad_ref[...] + functools.reduce(jnp.add, parts)

    attend([dict(
        fox_scores=lambda r=r: jnp.dot(qf_ref[0], cfk[r][0, 0].reshape(W_BR, page).astype(BF16),
                                       preferred_element_type=F32),
        fox_pv=lambda p, r=r: _nt_dot(p, cfv[r][0, 0].reshape(W_BR, page).astype(BF16)),
        fl=cfl[r][0, 0],
        dk=cdk[r][0, 0].reshape(nd, dv).astype(BF16),
        dv=cdv[r][0, 0].reshape(nd, dv).astype(BF16),
        pos0=(step_id * pp + r - n_steps * pp) * page) for r in range(pp)], False)

    @pl.when(step_id == n_steps - 1)
    def _():
        def pad_rows(x, n):
            return jnp.concatenate([x, jnp.zeros((n - x.shape[0], x.shape[1]), x.dtype)], axis=0).astype(BF16)

        attend([dict(
            fox_scores=lambda: _nt_dot(qf_ref[0], pad_rows(nfk_ref[0], page)),
            fox_pv=lambda p: jnp.dot(p, pad_rows(nfv_ref[0], page), preferred_element_type=F32),
            fl=nfl_ref[0],
            dk=pad_rows(ndk_ref[0].reshape(t_new * H_DIFF, dv), nd),
            dv=pad_rows(ndv_ref[0].reshape(t_new * H_DIFF, dv), nd),
            pos0=0)], True)
        o3 = (af_ref[...] * (1.0 / _tile_lanes(lf_ref[...], W_BR // LANES))).reshape(H_FOX, t_new, W_BR)
        own = (lax.broadcasted_iota(jnp.int32, o3.shape, 0)
               == lax.broadcasted_iota(jnp.int32, o3.shape, 2) >> _log2(D_FOX))
        of_ref[0] = jnp.sum(jnp.where(own, o3, 0.0), axis=0).astype(of_ref.dtype)
        lam = _lambda_value(lam_ref, lam_init)
        o = ad_ref[...] * (1.0 / ld_ref[...])
        for h in range(H_DIFF):
            o1 = o[h * 2 * t_new:h * 2 * t_new + t_new]
            o2 = o[h * 2 * t_new + t_new:(h + 1) * 2 * t_new]
            od_ref[0, :, h * dv:(h + 1) * dv] = _rms_scale(o1 - lam * o2, g_ref[0], lam_init).astype(od_ref.dtype)


def _sample_attention(l, page_table, qf, qd, qm, caches, news, mem_k, mem_v, slopes, lamv, g, lam_init):
    cfk, cfv, cfl, cdk, cdv = caches
    n_seq, n_pages = page_table.shape
    page = cdk.shape[2]
    t_new = news[0].shape[1]
    n_mem = mem_k.shape[2]
    pp = PAGES_PER_STEP
    assert n_pages % pp == 0, (n_pages, pp)
    n_steps = n_pages // pp
    dv = 2 * D_DIFF

    def cache_spec(shape, r):
        return pl.BlockSpec((1, 1) + shape,
                            lambda s, p, pt, r=r: (l, pt[s * n_pages + p * pp + r]) + (0,) * len(shape))

    def seq_spec(shape):
        return pl.BlockSpec((1,) + shape, lambda s, p, pt: (s,) + (0,) * len(shape))

    layer3 = lambda s, p, pt: (l, 0, 0)
    mem_spec = pl.BlockSpec((1, 1, n_mem, H_MEM, D_MEM), lambda s, p, pt: (l, s, 0, 0, 0))
    rows = H_FOX * t_new
    in_specs = [seq_spec((rows, W_BR)), seq_spec((rows, dv)), seq_spec((H_MEM * t_new, D_MEM))]
    operands = [qf, qd, qm]
    for arr, shape in ((cfk, (H_FOX, D_FOX, page)), (cfv, (H_FOX, D_FOX, page)), (cfl, (H_FOX, page)),
                       (cdk, (page, H_DIFF, dv)), (cdv, (page, H_DIFF, dv))):
        for r in range(pp):
            in_specs.append(cache_spec(shape, r))
            operands.append(arr)
    in_specs += [seq_spec((t_new, W_BR)), seq_spec((t_new, W_BR)), seq_spec((H_FOX, page)),
                 seq_spec((t_new, H_DIFF, dv)), seq_spec((t_new, H_DIFF, dv)),
                 mem_spec, mem_spec,
                 pl.BlockSpec(memory_space=pltpu.SMEM),
                 pl.BlockSpec((1, 4, D_DIFF), layer3), pl.BlockSpec((1, 1, dv), layer3)]
    operands += list(news) + [mem_k, mem_v, slopes, lamv, g]
    grid_spec = pltpu.PrefetchScalarGridSpec(
        num_scalar_prefetch=1,
        grid=(n_seq, n_steps),
        in_specs=in_specs,
        out_specs=[seq_spec((t_new, W_BR)), seq_spec((t_new, W_BR)), seq_spec((t_new, W_BR))],
        scratch_shapes=[pltpu.VMEM((rows, LANES), F32), pltpu.VMEM((rows, LANES), F32),
                        pltpu.VMEM((rows, W_BR), F32),
                        pltpu.VMEM((rows, LANES), F32), pltpu.VMEM((rows, LANES), F32),
                        pltpu.VMEM((rows, dv), F32),
                        pltpu.VMEM((8, LANES), F32)])
    return pl.pallas_call(
        functools.partial(_sample_body, n_steps=n_steps, pp=pp, page=page, t_new=t_new, lam_init=lam_init),
        grid_spec=grid_spec,
        out_shape=[jax.ShapeDtypeStruct((n_seq, t_new, W_BR), BF16),
                   jax.ShapeDtypeStruct((n_seq, t_new, W_BR), BF16),
                   jax.ShapeDtypeStruct((n_seq, t_new, W_BR), BF16)],
        name="sample_attn",
        compiler_params=_cparams(("parallel", "arbitrary")),
    )(page_table.reshape(-1), *operands)


_A_FQ, _A_FG, _A_DQ, _A_DG, _A_MQ, _A_MG = range(6)
_A_GATES = 6 * W_BR
_KV_FK, _KV_FV, _KV_DK, _KV_DV = range(4)


def _split_w_in(w):
    sizes = (W_BR, W_BR, W_BR, W_BR, H_FOX, W_BR, W_BR, W_BR, W_BR, W_BR, W_BR, N_BRANCH * D_MODEL)
    offs = [0]
    for s in sizes:
        offs.append(offs[-1] + s)
    fq, fk, fv, fg, ff, dq, dk, dv, dg, mq, mg, gates = [w[..., offs[n]:offs[n + 1]] for n in range(12)]
    w_act = jnp.concatenate([fq * (D_FOX ** -0.5), fg, dq * (D_DIFF ** -0.5), dg, mq, mg, gates],
                            axis=-1).astype(BF16)
    w_kv = jnp.concatenate([fk, fv, dk, dv], axis=-1).astype(BF16)
    w_ff = jnp.pad(ff, ((0, 0), (0, 0), (0, LANES - H_FOX))).astype(BF16)
    return w_act, w_kv, w_ff


def kernel(x_prompt, x_sample, mem_prompt, cache_fox_k, cache_fox_v, cache_fox_logf, cache_diff_k,
           cache_diff_v, cache_mem_k, cache_mem_v, page_table, w_in, b_forget, w_mem_kv, lambda_q1,
           lambda_k1, lambda_q2, lambda_k2, diff_norm_g, w_proj_fox, w_proj_diff, w_proj_mem, w_out,
           ln_g, ln_b):
    depth = w_in.shape[0]
    b, t, _ = x_prompt.shape
    n_seq, t_new, _ = x_sample.shape
    page = cache_fox_k.shape[2]
    n_mem = mem_prompt.shape[1]
    alpha = (2 * depth) ** 0.25
    tq = min(1024, t)
    tk = min(512, t)
    dv = 2 * D_DIFF
    nblk = W_BR // LANES

    slopes = 2.0 ** (-8.0 * jnp.arange(1, H_DIFF + 1, dtype=F32) / H_DIFF)
    cfk = jnp.transpose(cache_fox_k, (0, 1, 3, 4, 2))
    cfv = jnp.transpose(cache_fox_v, (0, 1, 3, 4, 2))
    cfl = jnp.swapaxes(cache_fox_logf, 2, 3)
    caches = (cfk, cfv, cfl, cache_diff_k, cache_diff_v)
    lane_lo = (jnp.arange(dv) < D_DIFF)
    head_eye = (jnp.arange(H_FOX)[:, None, None] == jnp.arange(W_BR)[None, None, :] // D_FOX)

    w_act, w_kv, w_ff = _split_w_in(w_in)
    w_mkv = w_mem_kv.astype(BF16)
    b_ff = jnp.pad(b_forget, ((0, 0), (0, LANES - H_FOX))).reshape(depth, 1, LANES)
    lamv = jnp.stack([lambda_q1, lambda_k1, lambda_q2, lambda_k2], axis=1)
    g = diff_norm_g.reshape(depth, 1, dv)
    merge_w = (w_proj_fox.astype(BF16), w_proj_diff.astype(BF16), w_proj_mem.astype(BF16),
               w_out.astype(BF16), ln_g.reshape(depth, 1, D_MODEL), ln_b.reshape(depth, 1, D_MODEL))
    merge_cols = (_A_FG, _A_DG, _A_MG, _A_GATES // (N_BRANCH * D_MODEL))

    xp = x_prompt.reshape(b * t, D_MODEL)
    xs = x_sample.reshape(n_seq * t_new, D_MODEL)
    m_s = n_seq * t_new
    mem2 = mem_prompt.reshape(b * n_mem, D_MODEL)
    outs = [[] for _ in range(12)]
    kv_stacks = None

    for l in range(depth):
        lam_init = 0.8 - 0.6 * math.exp(-0.3 * l)

        act = _matmul(xp, w_act, l, 3072, BF16)
        kv_stacks, kv16 = _proj_kv(xp, w_kv, l, kv_stacks, b, t, 2)
        logf = _logf_proj(xp, w_ff, b_ff, l)[:, :H_FOX]
        mkv32, mkv16 = _matmul_dual(mem2, w_mkv, l, W_BR)
        fcum = _cumsum_seq(jnp.swapaxes(logf.reshape(b, t, H_FOX), 1, 2).reshape(b * H_FOX, t))
        act3 = act.reshape(b, t, -1)
        kv3 = kv16.reshape(b, t, -1)
        o_fox = _fox_attention(act3, kv3, fcum.reshape(b, H_FOX, t // tk, tk), b, t,
                               _A_FQ * nblk, _KV_FK * nblk, _KV_FV * nblk, tq, tk)
        o_diff = _diff_attention(act3, kv3, slopes, lamv, g, l, b, t,
                                 _A_DQ * nblk, _KV_DK * nblk, _KV_DV * nblk, lam_init, tq, tk)
        o_mem = _mem_attention(act3, mkv16.reshape(b, n_mem, 2 * W_BR), b, t, _A_MQ, tq=tk)
        xp = _merge(xp, o_fox.reshape(b * t, W_BR), o_diff.reshape(b * t, W_BR),
                    o_mem.reshape(b * t, W_BR), act, merge_cols, *merge_w, l, alpha)
        outs[2].append(logf.reshape(b, t, H_FOX))
        outs[5].append(mkv32[0].reshape(b, n_mem, H_MEM, D_MEM))
        outs[6].append(mkv32[1].reshape(b, n_mem, H_MEM, D_MEM))

        act = _matmul(xs, w_act, l, 3072, BF16)
        kv32 = _matmul(xs, w_kv, l, W_BR, F32).reshape(m_s, 4, W_BR)
        logf = _logf_proj(xs, w_ff, b_ff, l)[:, :H_FOX]
        new_fk = kv32[:, _KV_FK].reshape(n_seq, t_new, H_FOX, D_FOX)
        new_fv = kv32[:, _KV_FV].reshape(n_seq, t_new, H_FOX, D_FOX)
        new_dk = kv32[:, _KV_DK].reshape(n_seq, t_new, H_DIFF, dv)
        new_dv = kv32[:, _KV_DV].reshape(n_seq, t_new, H_DIFF, dv)
        new_fl = logf.reshape(n_seq, t_new, H_FOX)
        act4 = act.reshape(n_seq, t_new, -1)
        fq = act4[..., _A_FQ * W_BR:(_A_FQ + 1) * W_BR]
        qf = jnp.where(head_eye, fq[:, None], 0).reshape(n_seq, H_FOX * t_new, W_BR)
        dq = act4[..., _A_DQ * W_BR:(_A_DQ + 1) * W_BR].reshape(n_seq, t_new, H_DIFF, dv)
        dq2 = jnp.stack([jnp.where(lane_lo, dq, 0), jnp.where(lane_lo, 0, dq)], axis=0)
        qd = jnp.transpose(dq2, (1, 3, 0, 2, 4)).reshape(n_seq, H_DIFF * 2 * t_new, dv)
        qm = jnp.swapaxes(act4[..., _A_MQ * W_BR:(_A_MQ + 1) * W_BR].reshape(n_seq, t_new, H_MEM, D_MEM), 1, 2)
        qm = qm.reshape(n_seq, H_MEM * t_new, D_MEM)
        nfl = jnp.pad(jnp.swapaxes(new_fl, 1, 2), ((0, 0), (0, 0), (0, page - t_new)))
        o_fox, o_diff, o_mem = _sample_attention(
            l, page_table, qf, qd, qm, caches,
            (kv32[:, _KV_FK].reshape(n_seq, t_new, W_BR), kv32[:, _KV_FV].reshape(n_seq, t_new, W_BR),
             nfl, new_dk, new_dv),
            cache_mem_k, cache_mem_v, slopes, lamv, g, lam_init)
        xs = _merge(xs, o_fox.reshape(m_s, W_BR), o_diff.reshape(m_s, W_BR), o_mem.reshape(m_s, W_BR),
                    act, merge_cols, *merge_w, l, alpha)
        outs[7].append(new_fk)
        outs[8].append(new_fv)
        outs[9].append(new_fl)
        outs[10].append(new_dk)
        outs[11].append(new_dv)

    stacked = [jnp.stack(o) if o else None for o in outs]
    stacked[0] = jnp.transpose(kv_stacks[_KV_FK].reshape(depth, b, H_FOX, D_FOX, t), (0, 1, 4, 2, 3))
    stacked[1] = jnp.transpose(kv_stacks[_KV_FV].reshape(depth, b, H_FOX, D_FOX, t), (0, 1, 4, 2, 3))
    stacked[3] = kv_stacks[_KV_DK].reshape(depth, b, t, H_DIFF, dv)
    stacked[4] = kv_stacks[_KV_DV].reshape(depth, b, t, H_DIFF, dv)
    return (xp.reshape(b, t, D_MODEL), xs.reshape(n_seq, t_new, D_MODEL)) + tuple(stacked)
```
